```python
import math
import jax, jax.numpy as jnp
from jax import lax
import numpy as np

D_MODEL = 1024
BATCH = 1
SEQ = 16384
DEPTH = 1
DEC_BATCH = 128
DEC_SEQ = 8
PAST_LEN = 8192
PAGE_SIZE = 128

N_HEADS_A = 8
N_KV_A = 2
HEAD_DIM = 64
GROUP_R = N_HEADS_A // N_KV_A
ROT_DIM = HEAD_DIM // 4
ROPE_THETA = 500000.0
CMP_LEN = 32
CMP_STRIDE = 16
CMP_HID = 256
SEL_BLOCK = 64
N_SEL = 16
WINDOW = 512
Q_BLOCK = 128
FORCE_BONUS = 1.0e3
N_HEADS_M = 4
HEAD_DIM_M = 128
CONV_W = 4
MLSTM_CHUNK = 64
PEER_HEADS = 8
PEER_NKEYS = 128
PEER_TOPK = 16
PEER_DKEY = 256
N_EXPERTS = PEER_NKEYS * PEER_NKEYS
PEER_BLOCK = 128
A_WIDTH = N_HEADS_A * HEAD_DIM
M_WIDTH = N_HEADS_M * HEAD_DIM_M
KV_COLS = 2 * N_KV_A * HEAD_DIM
GATE_COLS = 3 * N_HEADS_A
SPLIT_SIZES = (A_WIDTH, KV_COLS, KV_COLS, KV_COLS, GATE_COLS, M_WIDTH, M_WIDTH, M_WIDTH, N_HEADS_M, N_HEADS_M)
IN_COLS = A_WIDTH + 3 * KV_COLS + GATE_COLS + 3 * M_WIDTH + 2 * N_HEADS_M
EPS = 1e-6
NEG = -1e30

kernel_name = "hymba_nsa_mlstm_peer_step"


def rms_norm(x, g):
    xf = x.astype(jnp.float32)
    y = xf * lax.rsqrt(jnp.mean(xf * xf, axis=-1, keepdims=True) + EPS)
    return (y * g.astype(jnp.float32)).astype(x.dtype)


def rotary(x, pos):
    inv = ROPE_THETA ** (-jnp.arange(0, ROT_DIM, 2, dtype=jnp.float32) / ROT_DIM)
    ang = pos.astype(jnp.float32)[:, None] * inv[None, :]
    cos = jnp.cos(ang)[:, None, :]
    sin = jnp.sin(ang)[:, None, :]
    half = ROT_DIM // 2
    x1 = x[..., :half].astype(jnp.float32)
    x2 = x[..., half:ROT_DIM].astype(jnp.float32)
    rot = jnp.concatenate([x1 * cos - x2 * sin, x2 * cos + x1 * sin], axis=-1).astype(x.dtype)
    return jnp.concatenate([rot, x[..., ROT_DIM:]], axis=-1)


def masked_softmax(s, mask, axis):
    s = jnp.where(mask, s, NEG)
    mx = jnp.max(s, axis=axis, keepdims=True)
    e = jnp.where(mask, jnp.exp(s - mx), 0.0)
    return e / jnp.maximum(jnp.sum(e, axis=axis, keepdims=True), 1e-30)


def compress_rows(rows, pe, w1, w2):
    z = rows + pe[None, :, None, :]
    hid = jax.nn.gelu(jnp.einsum('nlgd,ldh->ngh', z, w1.reshape(CMP_LEN, HEAD_DIM, CMP_HID)))
    return jnp.einsum('ngh,hd->ngd', hid, w2)


def nsa_context(kv_cmp, kv_slc, cmp_params):
    pe_k, w1_k, w2_k, pe_v, w1_v, w2_v = cmp_params
    t = kv_cmp.shape[0]
    nc = (t - CMP_LEN) // CMP_STRIDE + 1
    cstart = jnp.arange(nc) * CMP_STRIDE
    blocks = kv_cmp[cstart[:, None] + jnp.arange(CMP_LEN)[None, :]]
    kc = compress_rows(blocks[:, :, 0], pe_k, w1_k, w2_k)
    vc = compress_rows(blocks[:, :, 1], pe_v, w1_v, w2_v)
    cend = cstart + CMP_LEN - 1
    nb = -(-t // SEL_BLOCK)
    rows = jnp.pad(kv_slc, ((0, nb * SEL_BLOCK - t), (0, 0), (0, 0), (0, 0)))
    rows = rows.reshape(nb, SEL_BLOCK, 2, N_KV_A, HEAD_DIM)
    bstart = jnp.arange(nb) * SEL_BLOCK
    overlap = ((cstart[:, None] < bstart[None, :] + SEL_BLOCK)
               & (cstart[:, None] + CMP_LEN > bstart[None, :])).astype(jnp.float32)
    return kc, vc, cend, overlap, rows[:, :, 0], rows[:, :, 1]


def nsa_attend(q, qpos, gates, kc, vc, cend, overlap, kb, vb, kw, vw, wpos):
    nq = q.shape[0]
    scale = HEAD_DIM ** -0.5
    qg = q.reshape(nq, N_KV_A, GROUP_R, HEAD_DIM)
    s_c = jnp.einsum('qgrd,ngd->qgrn', qg, kc).astype(jnp.float32) * scale
    m_c = (cend[None, :] <= qpos[:, None])[:, None, None, :]
    p_c = masked_softmax(s_c, m_c, -1)
    o_c = jnp.einsum('qgrn,ngd->qgrd', p_c.astype(vc.dtype), vc)
    nb = kb.shape[0]
    blk = jnp.arange(nb)
    imp = jnp.einsum('qgn,nb->qgb', p_c.sum(axis=2), overlap)
    cur = qpos // SEL_BLOCK
    forced = (blk[None, :] == 0) | (blk[None, :] == cur[:, None]) | (blk[None, :] == cur[:, None] - 1)
    imp = imp + FORCE_BONUS * forced[:, None, :].astype(jnp.float32)
    avail = (blk[None, :] * SEL_BLOCK <= qpos[:, None])[:, None, :]
    imp = jnp.where(avail, imp, NEG)
    _, sel = lax.top_k(imp, min(N_SEL, nb))
    gi = jnp.arange(N_KV_A)[None, :, None]
    k_sel = jnp.transpose(kb, (2, 0, 1, 3))[gi, sel]
    v_sel = jnp.transpose(vb, (2, 0, 1, 3))[gi, sel]
    s_s = jnp.einsum('qgrd,qgksd->qgrks', qg, k_sel).astype(jnp.float32) * scale
    tpos = sel[..., None] * SEL_BLOCK + jnp.arange(SEL_BLOCK)
    m_s = (tpos <= qpos[:, None, None, None])[:, :, None]
    p_s = masked_softmax(s_s, m_s, (-2, -1))
    o_s = jnp.einsum('qgrks,qgksd->qgrd', p_s.astype(v_sel.dtype), v_sel)
    s_w = jnp.einsum('qgrd,wgd->qgrw', qg, kw).astype(jnp.float32) * scale
    dpos = qpos[:, None] - wpos[None, :]
    m_w = ((dpos >= 0) & (dpos < WINDOW) & (wpos[None, :] >= 0))[:, None, None, :]
    p_w = masked_softmax(s_w, m_w, -1)
    o_w = jnp.einsum('qgrw,wgd->qgrd', p_w.astype(vw.dtype), vw)
    g = gates.reshape(nq, 3, N_KV_A, GROUP_R, 1)
    o = g[:, 0] * o_c + g[:, 1] * o_s + g[:, 2] * o_w
    return o.reshape(nq, N_HEADS_A, HEAD_DIM)


def nsa_prompt(q, kv_cmp, kv_slc, kv_win, gates, cmp_params):
    def one_seq(args):
        qs, kvc, kvs, kvw, gs = args
        kc, vc, cend, overlap, kb, vb = nsa_context(kvc, kvs, cmp_params)
        kw_pad = jnp.pad(kvw, ((WINDOW, 0), (0, 0), (0, 0), (0, 0)))
        s_len = qs.shape[0]

        def q_block(i):
            start = i * Q_BLOCK
            qb = lax.dynamic_slice_in_dim(qs, start, Q_BLOCK, 0)
            gb = lax.dynamic_slice_in_dim(gs, start, Q_BLOCK, 0)
            wb = lax.dynamic_slice_in_dim(kw_pad, start, WINDOW + Q_BLOCK, 0)
            qpos = start + jnp.arange(Q_BLOCK)
            wpos = start - WINDOW + jnp.arange(WINDOW + Q_BLOCK)
            return nsa_attend(qb, qpos, gb, kc, vc, cend, overlap, kb, vb, wb[:, 0], wb[:, 1], wpos)

        out = lax.map(q_block, jnp.arange(s_len // Q_BLOCK))
        return out.reshape(s_len, N_HEADS_A, HEAD_DIM)

    return lax.map(one_seq, (q, kv_cmp, kv_slc, kv_win, gates))


def nsa_sample(q, kv_cmp, kv_slc, kv_win, gates, cache_cmp_kv, cache_slc_kv, page_table, state_win_kv, cmp_params):
    def one_seq(args):
        qs, kvc, kvs, kvw, gs, pages, wbuf = args
        row_shape = (-1,) + cache_cmp_kv.shape[2:]
        past_c = cache_cmp_kv[pages].reshape(row_shape)
        past_s = cache_slc_kv[pages].reshape(row_shape)
        kc, vc, cend, overlap, kb, vb = nsa_context(
            jnp.concatenate([past_c, kvc.astype(past_c.dtype)], axis=0),
            jnp.concatenate([past_s, kvs.astype(past_s.dtype)], axis=0), cmp_params)
        kw = jnp.concatenate([wbuf, kvw.astype(wbuf.dtype)], axis=0)
        past_len = past_c.shape[0]
        qpos = past_len + jnp.arange(qs.shape[0])
        wpos = past_len - wbuf.shape[0] + jnp.arange(kw.shape[0])
        return nsa_attend(qs, qpos, gs, kc, vc, cend, overlap, kb, vb, kw[:, 0], kw[:, 1], wpos)

    return lax.map(one_seq, (q, kv_cmp, kv_slc, kv_win, gates, page_table, state_win_kv))


def short_conv(u, buf, w, b):
    t = u.shape[1]
    ext = jnp.concatenate([buf.astype(u.dtype), u], axis=1)
    out = b + ext[:, 0:t] * w[0]
    for j in range(1, CONV_W):
        out = out + ext[:, j:j + t] * w[j]
    return out, ext[:, t:]


def mlstm_chunk_step(carry, inp):
    c_prev, n_prev, m_prev = carry
    q, k, v, ig, lf = inp
    L = q.shape[2]
    fcum = jnp.cumsum(lf, axis=-1)
    m_row = fcum + jnp.maximum(m_prev[..., None], lax.cummax(ig - fcum, axis=2))
    causal = jnp.tril(jnp.ones((L, L), dtype=bool))
    logw = fcum[..., :, None] - fcum[..., None, :] + ig[..., None, :] - m_row[..., :, None]
    w_intra = jnp.exp(jnp.where(causal, logw, NEG))
    w_inter = jnp.exp(m_prev[..., None] + fcum - m_row)
    qk = jnp.einsum('bhtd,bhsd->bhts', q, k) * w_intra
    num = w_inter[..., None] * jnp.einsum('bhtd,bhde->bhte', q, c_prev) + jnp.einsum('bhts,bhse->bhte', qk, v)
    den = w_inter * jnp.einsum('bhtd,bhd->bht', q, n_prev) + jnp.sum(qk, axis=-1)
    h = num / jnp.maximum(jnp.abs(den), jnp.exp(-m_row))[..., None]
    m_new = m_row[..., -1]
    w_keep = jnp.exp(m_prev + fcum[..., -1] - m_new)
    w_src = jnp.exp(ig + fcum[..., -1:] - fcum - m_new[..., None])
    c_new = w_keep[..., None, None] * c_prev + jnp.einsum('bhs,bhsd,bhse->bhde', w_src, k, v)
    n_new = w_keep[..., None] * n_prev + jnp.einsum('bhs,bhsd->bhd', w_src, k)
    return (c_new, n_new, m_new), h


def mlstm_scan(q, k, v, ig, lf, c0, n0, m0, chunk):
    b, h, t, d = q.shape
    nck = t // chunk

    def chunks(a):
        return jnp.moveaxis(a.reshape((b, h, nck, chunk) + a.shape[3:]), 2, 0)

    carry, hs = lax.scan(mlstm_chunk_step, (c0, n0, m0),
                         (chunks(q), chunks(k), chunks(v), chunks(ig), chunks(lf)))
    return carry, jnp.moveaxis(hs, 0, 2).reshape(b, h, t, d)


def peer_ffn(h, wq, keys, u_tab, v_tab):
    n_tok, d = h.shape
    nblk = -(-n_tok // PEER_BLOCK)
    hp = jnp.pad(h, ((0, nblk * PEER_BLOCK - n_tok), (0, 0))).reshape(nblk, PEER_BLOCK, d)

    def block(xb):
        q = (xb @ wq).reshape(PEER_BLOCK, PEER_HEADS, 2, PEER_DKEY // 2)
        s = jnp.einsum('thcd,hcnd->thcn', q, keys).astype(jnp.float32)
        s1, i1 = lax.top_k(s[:, :, 0], PEER_TOPK)
        s2, i2 = lax.top_k(s[:, :, 1], PEER_TOPK)
        cand_s = (s1[..., :, None] + s2[..., None, :]).reshape(PEER_BLOCK, PEER_HEADS, PEER_TOPK * PEER_TOPK)
        cand_i = (i1[..., :, None] * PEER_NKEYS + i2[..., None, :]).reshape(PEER_BLOCK, PEER_HEADS, PEER_TOPK * PEER_TOPK)
        top_s, top_p = lax.top_k(cand_s, PEER_TOPK)
        eidx = jnp.take_along_axis(cand_i, top_p, axis=-1)
        g = jax.nn.softmax(top_s, axis=-1)
        act = jax.nn.gelu(jnp.einsum('thkd,td->thk', u_tab[eidx], xb).astype(jnp.float32))
        return jnp.einsum('thk,thkd->td', (g * act).astype(v_tab.dtype), v_tab[eidx])

    return lax.map(block, hp).reshape(nblk * PEER_BLOCK, d)[:n_tok]


def setup_inputs(seed: int = 0) -> dict:
    key = jax.random.key(seed)
    ks = jax.random.split(key, 40)
    f32 = jnp.float32

    def nrm(k, shape, scale):
        return scale * jax.random.normal(k, shape, f32)

    n_pages = PAST_LEN // PAGE_SIZE
    n_pool = (5 * DEC_BATCH * n_pages + 3) // 4
    win_len = min(WINDOW, PAST_LEN)
    kv_row = (2, N_KV_A, HEAD_DIM)
    page_table = jax.random.permutation(ks[4], n_pool)[: DEC_BATCH * n_pages].reshape(DEC_BATCH, n_pages).astype(jnp.int32)
    return {
        "x_prompt": nrm(ks[0], (BATCH, SEQ, D_MODEL), 1.0),
        "x_sample": nrm(ks[1], (DEC_BATCH, DEC_SEQ, D_MODEL), 1.0),
        "cache_cmp_kv": nrm(ks[2], (n_pool, PAGE_SIZE) + kv_row, 1.0),
        "cache_slc_kv": nrm(ks[3], (n_pool, PAGE_SIZE) + kv_row, 1.0),
        "page_table": page_table,
        "state_win_kv": nrm(ks[5], (DEC_BATCH, win_len) + kv_row, 1.0),
        "state_C": nrm(ks[6], (DEC_BATCH, N_HEADS_M, HEAD_DIM_M, HEAD_DIM_M), 0.1),
        "state_n": nrm(ks[7], (DEC_BATCH, N_HEADS_M, HEAD_DIM_M), 0.1),
        "state_m": nrm(ks[8], (DEC_BATCH, N_HEADS_M), 1.0),
        "state_conv": nrm(ks[9], (DEC_BATCH, CONV_W - 1, M_WIDTH), 1.0),
        "c_prompt": nrm(ks[10], (BATCH, D_MODEL), 1.0),
        "c_sample": nrm(ks[11], (DEC_BATCH, D_MODEL), 1.0),
        "w_ada": nrm(ks[12], (D_MODEL, 6 * D_MODEL), 0.5 * D_MODEL ** -0.5),
        "b_ada": nrm(ks[13], (6 * D_MODEL,), 0.02),
        "norm1_g": 1.0 + nrm(ks[14], (D_MODEL,), 0.02),
        "w_in": nrm(ks[15], (D_MODEL, IN_COLS), D_MODEL ** -0.5),
        "b_gate_a": nrm(ks[16], (GATE_COLS,), 0.1),
        "cmp_pe_k": nrm(ks[17], (CMP_LEN, HEAD_DIM), 0.1),
        "cmp_w1_k": nrm(ks[18], (CMP_LEN * HEAD_DIM, CMP_HID), (CMP_LEN * HEAD_DIM) ** -0.5),
        "cmp_w2_k": nrm(ks[19], (CMP_HID, HEAD_DIM), CMP_HID ** -0.5),
        "cmp_pe_v": nrm(ks[20], (CMP_LEN, HEAD_DIM), 0.1),
        "cmp_w1_v": nrm(ks[21], (CMP_LEN * HEAD_DIM, CMP_HID), (CMP_LEN * HEAD_DIM) ** -0.5),
        "cmp_w2_v": nrm(ks[22], (CMP_HID, HEAD_DIM), CMP_HID ** -0.5),
        "conv_w": nrm(ks[23], (CONV_W, M_WIDTH), 0.5),
        "conv_b": nrm(ks[24], (M_WIDTH,), 0.02),
        "w_mq": nrm(ks[25], (N_HEADS_M, HEAD_DIM_M, HEAD_DIM_M), HEAD_DIM_M ** -0.5),
        "w_mk": nrm(ks[26], (N_HEADS_M, HEAD_DIM_M, HEAD_DIM_M), HEAD_DIM_M ** -0.5),
        "b_i": nrm(ks[27], (N_HEADS_M,), 0.1),
        "b_f": 3.0 + nrm(ks[28], (N_HEADS_M,), 0.5),
        "mnorm_g": 1.0 + nrm(ks[29], (M_WIDTH,), 0.02),
        "w_out": nrm(ks[30], (D_MODEL, D_MODEL), D_MODEL ** -0.5),
        "norm2_g": 1.0 + nrm(ks[31], (D_MODEL,), 0.02),
        "peer_wq": nrm(ks[32], (D_MODEL, PEER_HEADS * PEER_DKEY), D_MODEL ** -0.5),
        "peer_keys": nrm(ks[33], (PEER_HEADS, 2, PEER_NKEYS, PEER_DKEY // 2), (PEER_DKEY // 2) ** -0.5),
        "peer_u": nrm(ks[34], (N_EXPERTS, D_MODEL), D_MODEL ** -0.5),
        "peer_v": nrm(ks[35], (N_EXPERTS, D_MODEL), 0.5),
        "normf_g": 1.0 + nrm(ks[36], (D_MODEL,), 0.02),
    }


def reference(x_prompt, x_sample, cache_cmp_kv, cache_slc_kv, page_table, state_win_kv, state_C, state_n, state_m,
              state_conv, c_prompt, c_sample, w_ada, b_ada, norm1_g, w_in, b_gate_a, cmp_pe_k, cmp_w1_k, cmp_w2_k,
              cmp_pe_v, cmp_w1_v, cmp_w2_v, conv_w, conv_b, w_mq, w_mk, b_i, b_f, mnorm_g, w_out, norm2_g,
              peer_wq, peer_keys, peer_u, peer_v, normf_g):
    cmp_params = (cmp_pe_k, cmp_w1_k, cmp_w2_k, cmp_pe_v, cmp_w1_v, cmp_w2_v)
    split_at = np.cumsum(SPLIT_SIZES)[:-1].tolist()

    def group(x, c, pos, attend, conv_buf, c0, n0, m0, chunk):
        b, t, d = x.shape
        ada = jax.nn.silu(c) @ w_ada + b_ada
        sh1, sc1, g1, sh2, sc2, g2 = jnp.split(ada[:, None, :], 6, axis=-1)
        h = rms_norm(x, norm1_g) * (1.0 + sc1) + sh1
        z = h @ w_in
        q, kvc, kvs, kvw, gpre, u, vm, om, ipre, fpre = jnp.split(z, split_at, axis=-1)
        q = rotary(q.reshape(b, t, N_HEADS_A, HEAD_DIM), pos)

        def kv_rows(a):
            a = a.reshape(b, t, 2, N_KV_A, HEAD_DIM)
            return jnp.stack([rotary(a[:, :, 0], pos), a[:, :, 1]], axis=2)

        kvc, kvs, kvw = kv_rows(kvc), kv_rows(kvs), kv_rows(kvw)
        gates = jax.nn.sigmoid(gpre + b_gate_a).reshape(b, t, 3, N_HEADS_A)
        o_a = attend(q, kvc, kvs, kvw, gates)
        xc, conv_new = short_conv(u, conv_buf, conv_w, conv_b)
        xc = jax.nn.silu(xc).reshape(b, t, N_HEADS_M, HEAD_DIM_M).astype(jnp.float32)
        qm = jnp.einsum('bthd,hde->bhte', xc, w_mq.astype(jnp.float32))
        km = jnp.einsum('bthd,hde->bhte', xc, w_mk.astype(jnp.float32)) * HEAD_DIM_M ** -0.5
        vmh = vm.reshape(b, t, N_HEADS_M, HEAD_DIM_M).transpose(0, 2, 1, 3).astype(jnp.float32)
        ig = (ipre + b_i).astype(jnp.float32).transpose(0, 2, 1)
        lf = jax.nn.log_sigmoid((fpre + b_f).astype(jnp.float32)).transpose(0, 2, 1)
        (c_new, n_new, m_new), hm = mlstm_scan(qm, km, vmh, ig, lf, c0, n0, m0, chunk)
        hm = rms_norm(hm.transpose(0, 2, 1, 3), mnorm_g.reshape(N_HEADS_M, HEAD_DIM_M))
        hm = hm * jax.nn.sigmoid(om.astype(jnp.float32)).reshape(b, t, N_HEADS_M, HEAD_DIM_M)
        mix = jnp.concatenate([o_a.reshape(b, t, A_WIDTH), hm.reshape(b, t, M_WIDTH).astype(o_a.dtype)], axis=-1)
        x = x + g1 * (mix @ w_out)
        h2 = rms_norm(x, norm2_g) * (1.0 + sc2) + sh2
        x = x + g2 * peer_ffn(h2.reshape(b * t, d), peer_wq, peer_keys, peer_u, peer_v).reshape(b, t, d)
        y = rms_norm(x, normf_g)
        return y, kvc, kvs, kvw, conv_new, c_new, n_new, m_new

    b_p, t_p, _ = x_prompt.shape
    attend_p = lambda q, kvc, kvs, kvw, g: nsa_prompt(q, kvc, kvs, kvw, g, cmp_params)
    y_p, kvc_p, kvs_p, kvw_p, conv_p, C_p, n_p, m_p = group(
        x_prompt, c_prompt, jnp.arange(t_p), attend_p,
        jnp.zeros((b_p, CONV_W - 1, M_WIDTH), x_prompt.dtype),
        jnp.zeros((b_p, N_HEADS_M, HEAD_DIM_M, HEAD_DIM_M), jnp.float32),
        jnp.zeros((b_p, N_HEADS_M, HEAD_DIM_M), jnp.float32),
        jnp.zeros((b_p, N_HEADS_M), jnp.float32),
        min(MLSTM_CHUNK, t_p))

    t_s = x_sample.shape[1]
    past_len = page_table.shape[1] * cache_cmp_kv.shape[1]
    attend_s = lambda q, kvc, kvs, kvw, g: nsa_sample(q, kvc, kvs, kvw, g, cache_cmp_kv, cache_slc_kv,
                                                      page_table, state_win_kv, cmp_params)
    y_s, kvc_s, kvs_s, kvw_s, conv_s, C_s, n_s, m_s = group(
        x_sample, c_sample, past_len + jnp.arange(t_s), attend_s, state_conv,
        state_C.astype(jnp.float32), state_n.astype(jnp.float32), state_m.astype(jnp.float32), t_s)

    wb_len = state_win_kv.shape[1]
    win_p = jnp.concatenate([jnp.zeros((b_p, wb_len) + kvw_p.shape[2:], kvw_p.dtype), kvw_p], axis=1)[:, -wb_len:]
    win_s = jnp.concatenate([state_win_kv.astype(kvw_s.dtype), kvw_s], axis=1)[:, -wb_len:]
    return (y_p, y_s, kvc_p, kvc_s, kvs_p, kvs_s, win_p, win_s, C_p, C_s, n_p, n_s, m_p, m_s, conv_p, conv_s)
```

```python
import functools

import jax
import jax.numpy as jnp
from jax import lax
from jax.experimental import pallas as pl
from jax.experimental.pallas import tpu as pltpu

F32 = jnp.float32
BF16 = jnp.bfloat16

N_HEADS_A = 8
N_KV_A = 2
HEAD_DIM = 64
GROUP_R = N_HEADS_A // N_KV_A
ROT_DIM = HEAD_DIM // 4
ROPE_THETA = 500000.0
CMP_LEN = 32
CMP_STRIDE = 16
CMP_HID = 256
SEL_BLOCK = 64
N_SEL = 16
WINDOW = 512
Q_BLOCK = 128
FORCE_BONUS = 1.0e3
N_HEADS_M = 4
HEAD_DIM_M = 128
CONV_W = 4
PEER_HEADS = 8
PEER_NKEYS = 128
PEER_TOPK = 16
PEER_DKEY = 256
EPS = 1e-6
NEG = -1e30

A_WIDTH = N_HEADS_A * HEAD_DIM
M_WIDTH = N_HEADS_M * HEAD_DIM_M
KV_COLS = 2 * N_KV_A * HEAD_DIM
GATE_COLS = 3 * N_HEADS_A
KV_HALF = N_KV_A * HEAD_DIM
LANES = 128
SMALL_COLS = LANES
VMEM_LIMIT = 56 * 1024 * 1024


def _dot(a, b):
    return jnp.dot(a, b, preferred_element_type=F32)


def _dot_nt(a, b):
    return lax.dot_general(a, b, (((1,), (1,)), ((), ())), preferred_element_type=F32)


def _dot_tn(a, b):
    return lax.dot_general(a, b, (((0,), (0,)), ((), ())), preferred_element_type=F32)


def _params(*sem):
    return pltpu.CompilerParams(dimension_semantics=sem, vmem_limit_bytes=VMEM_LIMIT)


def _ada_kernel(c_ref, w_ref, b_ref, o_ref):
    c = c_ref[...]
    s = (c * jax.nn.sigmoid(c)).astype(BF16)
    o_ref[...] = _dot(s, w_ref[...].astype(BF16)) + b_ref[...]


def _ada(c, w_ada, b_ada):
    m, d = c.shape
    n = w_ada.shape[1]
    tn = n // 6
    return pl.pallas_call(
        _ada_kernel,
        grid=(n // tn,),
        in_specs=[pl.BlockSpec((m, d), lambda j: (0, 0)),
                  pl.BlockSpec((d, tn), lambda j: (0, j)),
                  pl.BlockSpec((1, tn), lambda j: (0, j))],
        out_specs=pl.BlockSpec((m, tn), lambda j: (0, j)),
        out_shape=jax.ShapeDtypeStruct((m, n), F32),
        compiler_params=_params("arbitrary"),
        name="ada",
    )(c, w_ada, b_ada.reshape(1, n))


IN_OFF_Q = 0
IN_OFF_KVC = A_WIDTH
IN_OFF_KVS = IN_OFF_KVC + KV_COLS
IN_OFF_KVW = IN_OFF_KVS + KV_COLS
IN_OFF_U = IN_OFF_KVW + KV_COLS
IN_OFF_VM = IN_OFF_U + M_WIDTH
IN_OFF_OM = IN_OFF_VM + M_WIDTH
IN_OFF_SMALL = IN_OFF_OM + M_WIDTH
IN_COLS_PADDED = IN_OFF_SMALL + SMALL_COLS
SM_IG = GATE_COLS
SM_FG = GATE_COLS + N_HEADS_M


def _permute_w_in(w_in, b_gate_a, b_i, b_f):
    d = w_in.shape[0]
    o = 0
    parts = {}
    for name, width in (("q", A_WIDTH), ("kvc", KV_COLS), ("kvs", KV_COLS), ("kvw", KV_COLS), ("g", GATE_COLS),
                        ("u", M_WIDTH), ("vm", M_WIDTH), ("om", M_WIDTH), ("i", N_HEADS_M), ("f", N_HEADS_M)):
        parts[name] = w_in[:, o:o + width]
        o += width
    pad = jnp.zeros((d, SMALL_COLS - GATE_COLS - 2 * N_HEADS_M), w_in.dtype)
    w = jnp.concatenate([parts[k] for k in ("q", "kvc", "kvs", "kvw", "u", "vm", "om", "g", "i", "f")] + [pad], axis=1)
    bias = jnp.concatenate([b_gate_a, b_i, b_f, jnp.zeros((SMALL_COLS - GATE_COLS - 2 * N_HEADS_M,), F32)])
    return w.astype(BF16), bias.reshape(1, SMALL_COLS)


def _rot_tables(pos):
    t = pos.shape[0]
    inv = ROPE_THETA ** (-jnp.arange(0, ROT_DIM, 2, dtype=F32) / ROT_DIM)
    ang = pos.astype(F32)[:, None] * inv[None, :]
    cos, sin = jnp.cos(ang), jnp.sin(ang)
    half = ROT_DIM // 2
    ones = jnp.ones((t, HEAD_DIM - ROT_DIM), F32)
    zeros = jnp.zeros((t, HEAD_DIM - ROT_DIM), F32)
    zh = jnp.zeros((t, half), F32)
    c = jnp.concatenate([cos, cos, ones], axis=1)
    s_lo = jnp.concatenate([-sin, zh, zeros], axis=1)
    s_hi = jnp.concatenate([zh, sin, zeros], axis=1)
    rep = LANES // HEAD_DIM
    return jnp.tile(c, (1, rep)), jnp.tile(s_lo, (1, rep)), jnp.tile(s_hi, (1, rep))


def _inproj_kernel(x_ref, g_ref, sc_ref, sh_ref, w_ref, bsm_ref, rc_ref, rlo_ref, rhi_ref,
                   q_ref, kvc_ref, kvs_ref, kvw_ref, u_ref, vm_ref, om_ref, sm_ref):
    x = x_ref[...]
    y = x * lax.rsqrt(jnp.mean(x * x, axis=-1, keepdims=True) + EPS) * g_ref[...]
    hb = (y * (1.0 + sc_ref[...]) + sh_ref[...]).astype(BF16)
    rc, rlo, rhi = rc_ref[...], rlo_ref[...], rhi_ref[...]
    half = ROT_DIM // 2

    def rot(z):
        return z * rc + pltpu.roll(z, LANES - half, 1) * rlo + pltpu.roll(z, half, 1) * rhi

    zq = _dot(hb, w_ref[:, IN_OFF_Q:IN_OFF_Q + A_WIDTH])
    scale = HEAD_DIM ** -0.5
    for c in range(A_WIDTH // LANES):
        q_ref[:, c * LANES:(c + 1) * LANES] = (rot(zq[:, c * LANES:(c + 1) * LANES]) * scale).astype(BF16)
    for ref, off in ((kvc_ref, IN_OFF_KVC), (kvs_ref, IN_OFF_KVS), (kvw_ref, IN_OFF_KVW)):
        z = _dot(hb, w_ref[:, off:off + KV_COLS])
        ref[:, 0:KV_HALF] = rot(z[:, 0:KV_HALF])
        ref[:, KV_HALF:KV_COLS] = z[:, KV_HALF:KV_COLS]
    u_ref[...] = _dot(hb, w_ref[:, IN_OFF_U:IN_OFF_U + M_WIDTH])
    vm_ref[...] = _dot(hb, w_ref[:, IN_OFF_VM:IN_OFF_VM + M_WIDTH])
    om_ref[...] = _dot(hb, w_ref[:, IN_OFF_OM:IN_OFF_OM + M_WIDTH])
    zs = _dot(hb, w_ref[:, IN_OFF_SMALL:IN_OFF_SMALL + SMALL_COLS]) + bsm_ref[...]
    lane = lax.broadcasted_iota(jnp.int32, zs.shape, 1)
    sm_ref[...] = jnp.where(lane < GATE_COLS, jax.nn.sigmoid(zs), zs)


def _inproj(x, norm_g, sc, sh, w_in_p, bias_small, rot_tabs, tm):
    r, d = x.shape
    per_row = sc.shape[0] != 1
    mod_spec = pl.BlockSpec((tm, d), lambda i: (i, 0)) if per_row else pl.BlockSpec((1, d), lambda i: (0, 0))
    row = lambda w: pl.BlockSpec((tm, w), lambda i: (i, 0))
    const = lambda a: pl.BlockSpec(a.shape, lambda i: (0, 0))
    out_shape = (jax.ShapeDtypeStruct((r, A_WIDTH), BF16),
                 jax.ShapeDtypeStruct((r, KV_COLS), F32), jax.ShapeDtypeStruct((r, KV_COLS), F32),
                 jax.ShapeDtypeStruct((r, KV_COLS), F32),
                 jax.ShapeDtypeStruct((r, M_WIDTH), F32), jax.ShapeDtypeStruct((r, M_WIDTH), F32),
                 jax.ShapeDtypeStruct((r, M_WIDTH), F32), jax.ShapeDtypeStruct((r, SMALL_COLS), F32))
    return pl.pallas_call(
        _inproj_kernel,
        grid=(r // tm,),
        in_specs=[row(d), const(norm_g), mod_spec, mod_spec, const(w_in_p), const(bias_small),
                  row(LANES), row(LANES), row(LANES)],
        out_specs=(row(A_WIDTH), row(KV_COLS), row(KV_COLS), row(KV_COLS), row(M_WIDTH), row(M_WIDTH),
                   row(M_WIDTH), row(SMALL_COLS)),
        out_shape=out_shape,
        compiler_params=_params("arbitrary"),
        name="inproj",
    )(x, norm_g, sc, sh, w_in_p, bias_small, *rot_tabs)


def _page_copies(pt_ref, pool_ref, kbuf, vbuf, sem, seq, slot, p):
    page_rows = pool_ref.shape[1]
    page = pt_ref[seq, p]
    dst = pl.ds(pl.multiple_of(p * page_rows, page_rows), page_rows)
    ck = pltpu.make_async_copy(pool_ref.at[page, :, pl.ds(0, KV_HALF)], kbuf.at[slot, dst, :], sem.at[slot, 0])
    cv = pltpu.make_async_copy(pool_ref.at[page, :, pl.ds(KV_HALF, KV_HALF)], vbuf.at[slot, dst, :], sem.at[slot, 1])
    return ck, cv


def _gather_start(pt_ref, pool_ref, kbuf, vbuf, sem, seq, slot):
    def body(p, _):
        ck, cv = _page_copies(pt_ref, pool_ref, kbuf, vbuf, sem, seq, slot, p)
        ck.start()
        cv.start()
        return 0
    lax.fori_loop(0, pt_ref.shape[1], body, 0)


def _gather_wait(pt_ref, pool_ref, kbuf, vbuf, sem, seq, slot):
    def body(p, _):
        ck, cv = _page_copies(pt_ref, pool_ref, kbuf, vbuf, sem, seq, slot, p)
        ck.wait()
        cv.wait()
        return 0
    lax.fori_loop(0, pt_ref.shape[1], body, 0)


def _gather_pipeline(pt_ref, pool_ref, kbuf, vbuf, sem):
    b = pl.program_id(0)
    nb = pl.num_programs(0)
    n_slots = kbuf.shape[0]
    slot = b % n_slots

    @pl.when(b == 0)
    def _():
        _gather_start(pt_ref, pool_ref, kbuf, vbuf, sem, 0, 0)

    if n_slots > 1:
        @pl.when(b + 1 < nb)
        def _():
            _gather_start(pt_ref, pool_ref, kbuf, vbuf, sem, b + 1, 1 - slot)

    _gather_wait(pt_ref, pool_ref, kbuf, vbuf, sem, b, slot)
    return slot


CMP_UNIT = CMP_STRIDE
CMP_SLAB = 256


def _compress_weights(pe_k, w1_k, w2_k, pe_v, w1_v, w2_v):
    eye = jnp.eye(N_KV_A, dtype=F32)

    def split_w1(w1):
        w = w1.reshape(CMP_LEN, HEAD_DIM, CMP_HID)

        def bd(part):
            return jnp.einsum('ldh,gk->lgdkh', part, eye).reshape(CMP_UNIT * KV_HALF, N_KV_A * CMP_HID).astype(BF16)
        return bd(w[:CMP_UNIT]), bd(w[CMP_UNIT:])

    def bd_w2(w2):
        return jnp.einsum('hd,gk->ghkd', w2, eye).reshape(N_KV_A * CMP_HID, KV_HALF)

    wka, wkb = split_w1(w1_k)
    wva, wvb = split_w1(w1_v)
    pe_a = jnp.concatenate([jnp.tile(pe_k[:CMP_UNIT], (1, N_KV_A)), jnp.tile(pe_v[:CMP_UNIT], (1, N_KV_A))], axis=1)
    pe_b = jnp.concatenate([jnp.tile(pe_k[CMP_UNIT:], (1, N_KV_A)), jnp.tile(pe_v[CMP_UNIT:], (1, N_KV_A))], axis=1)
    return (pe_a, pe_b, wka, wkb, wva, wvb, bd_w2(w2_k).astype(BF16), bd_w2(w2_v).T.astype(BF16))


def _compress_kernel(pt_ref, pool_ref, pea_ref, peb_ref, wka_ref, wkb_ref, wva_ref, wvb_ref, w2k_ref, w2vt_ref,
                     kc_ref, vct_ref, kbuf, vbuf, a_k, b_k, a_v, b_v, sem):
    slot = _gather_pipeline(pt_ref, pool_ref, kbuf, vbuf, sem)
    n_units = kc_ref.shape[1]
    slab = min(CMP_SLAB, n_units)
    for s in range(n_units // slab):
        base = s * slab * CMP_UNIT
        zka, zkb, zva, zvb = [], [], [], []
        for l in range(CMP_UNIT):
            xk = kbuf[slot, pl.ds(base + l, slab, stride=CMP_UNIT), :]
            xv = vbuf[slot, pl.ds(base + l, slab, stride=CMP_UNIT), :]
            zka.append((xk + pea_ref[l:l + 1, 0:KV_HALF]).astype(BF16))
            zkb.append((xk + peb_ref[l:l + 1, 0:KV_HALF]).astype(BF16))
            zva.append((xv + pea_ref[l:l + 1, KV_HALF:KV_COLS]).astype(BF16))
            zvb.append((xv + peb_ref[l:l + 1, KV_HALF:KV_COLS]).astype(BF16))
        rows = pl.ds(s * slab, slab)
        a_k[rows, :] = _dot(jnp.concatenate(zka, axis=1), wka_ref[...])
        b_k[rows, :] = _dot(jnp.concatenate(zkb, axis=1), wkb_ref[...])
        a_v[rows, :] = _dot(jnp.concatenate(zva, axis=1), wva_ref[...])
        b_v[rows, :] = _dot(jnp.concatenate(zvb, axis=1), wvb_ref[...])
    tail = pl.ds(n_units, 8)
    b_k[tail, :] = jnp.zeros((8, b_k.shape[1]), F32)
    b_v[tail, :] = jnp.zeros((8, b_v.shape[1]), F32)
    for s in range(n_units // slab):
        rows = pl.ds(s * slab, slab)
        nxt = pl.ds(s * slab + 1, slab)
        hid_k = jax.nn.gelu(a_k[rows, :] + b_k[nxt, :], approximate=True).astype(BF16)
        hid_v = jax.nn.gelu(a_v[rows, :] + b_v[nxt, :], approximate=True).astype(BF16)
        kc_ref[0, rows, :] = _dot(hid_k, w2k_ref[...]).astype(BF16)
        vct_ref[0, :, rows] = _dot_nt(w2vt_ref[...], hid_v).astype(BF16)


def _compress(page_table, pool, cw):
    b, n_pages = page_table.shape
    page_rows = pool.shape[1]
    rows = n_pages * page_rows
    n_units = rows // CMP_UNIT
    n_slots = min(2, b)
    hid2 = N_KV_A * CMP_HID
    const = lambda a: pl.BlockSpec(a.shape, lambda i, pt: (0,) * a.ndim)
    grid_spec = pltpu.PrefetchScalarGridSpec(
        num_scalar_prefetch=1,
        grid=(b,),
        in_specs=[pl.BlockSpec(memory_space=pl.ANY)] + [const(a) for a in cw],
        out_specs=(pl.BlockSpec((1, n_units, KV_HALF), lambda i, pt: (i, 0, 0)),
                   pl.BlockSpec((1, KV_HALF, n_units), lambda i, pt: (i, 0, 0))),
        scratch_shapes=[pltpu.VMEM((n_slots, rows, KV_HALF), F32), pltpu.VMEM((n_slots, rows, KV_HALF), F32),
                        pltpu.VMEM((n_units + 8, hid2), F32), pltpu.VMEM((n_units + 8, hid2), F32),
                        pltpu.VMEM((n_units + 8, hid2), F32), pltpu.VMEM((n_units + 8, hid2), F32),
                        pltpu.SemaphoreType.DMA((n_slots, 2))],
    )
    return pl.pallas_call(
        _compress_kernel,
        grid_spec=grid_spec,
        out_shape=(jax.ShapeDtypeStruct((b, n_units, KV_HALF), BF16),
                   jax.ShapeDtypeStruct((b, KV_HALF, n_units), BF16)),
        compiler_params=_params("arbitrary"),
        name="compress",
    )(page_table, pool, *cw)


REMOVED = -3.0e38


def _overlap_matrix(n_blocks, n_units):
    cstart = jnp.arange(n_units) * CMP_STRIDE
    bstart = jnp.arange(n_blocks) * SEL_BLOCK
    ov = (cstart[None, :] < bstart[:, None] + SEL_BLOCK) & (cstart[None, :] + CMP_LEN > bstart[:, None])
    return ov.astype(BF16)


def _dot_01(m01, x):
    hi = x.astype(BF16)
    r1 = x - hi.astype(F32)
    mid = r1.astype(BF16)
    lo = (r1 - mid.astype(F32)).astype(BF16)
    return _dot(m01, hi) + _dot(m01, mid) + _dot(m01, lo)


def _softmax_cols(s, mask):
    s = jnp.where(mask, s, NEG)
    mx = jnp.max(s, axis=0, keepdims=True)
    e = jnp.where(mask, jnp.exp(s - mx), 0.0)
    return e / jnp.maximum(jnp.sum(e, axis=0, keepdims=True), 1e-30)


def _online_init(rows, lanes):
    return (jnp.full((1, lanes), NEG, F32), jnp.zeros((1, lanes), F32), jnp.zeros((rows, lanes), F32))


def _online_update(carry, s, mask, pv):
    m_old, l_old, acc = carry
    s = jnp.where(mask, s, NEG)
    m_new = jnp.maximum(m_old, jnp.max(s, axis=0, keepdims=True))
    alpha = jnp.exp(m_old - m_new)
    p = jnp.where(mask, jnp.exp(s - m_new), 0.0)
    l_new = alpha * l_old + jnp.sum(p, axis=0, keepdims=True)
    return m_new, l_new, alpha * acc + pv(p.astype(BF16))


def _online_finish(carry):
    _, l, acc = carry
    return acc / jnp.maximum(l, 1e-30)


def _select_blocks(imp, qpos):
    blk = lax.broadcasted_iota(jnp.int32, imp.shape, 0)
    n_blocks = imp.shape[0]
    cur = qpos // SEL_BLOCK
    forced = (blk == 0) | (blk == cur) | (blk == cur - 1)
    imp = imp + jnp.where(forced, FORCE_BONUS, 0.0)
    work = jnp.where(blk * SEL_BLOCK <= qpos, imp, NEG)
    sel = jnp.zeros(imp.shape, F32)
    for _ in range(N_SEL):
        mx = jnp.max(work, axis=0, keepdims=True)
        first = jnp.min(jnp.where(work == mx, blk, n_blocks), axis=0, keepdims=True)
        hit = blk == first
        sel = jnp.where(hit, 1.0, sel)
        work = jnp.where(hit, REMOVED, work)
    return sel


def _block_mask(sel_ref, first_block, n, lanes_rep):
    parts = [jnp.broadcast_to(sel_ref[pl.ds(first_block + j, 1), :], (SEL_BLOCK, sel_ref.shape[1])) for j in range(n)]
    m = jnp.concatenate(parts, axis=0)
    if lanes_rep > 1:
        m = jnp.concatenate([m] * lanes_rep, axis=1)
    return m > 0.5


PROMPT_KV_CHUNK = 256
WIN_BLOCKS = WINDOW // Q_BLOCK + 1


def _nsa_prompt_kernel(qt_ref, kc_ref, vct_ref, ovl_ref, ks_ref, vst_ref, *rest):
    kw_refs = rest[0:WIN_BLOCKS]
    vwt_refs = rest[WIN_BLOCKS:2 * WIN_BLOCKS]
    gt_ref, o_ref, sel_ref = rest[2 * WIN_BLOCKS:]
    i = pl.program_id(0)
    start = i * Q_BLOCK
    lanes = GROUP_R * Q_BLOCK
    qpos1 = start + lax.broadcasted_iota(jnp.int32, (1, Q_BLOCK), 1)
    qpos = jnp.concatenate([qpos1] * GROUP_R, axis=1)
    n_units = kc_ref.shape[0]
    ck = PROMPT_KV_CHUNK
    blocks_per_chunk = ck // SEL_BLOCK
    for g in range(N_KV_A):
        top = jnp.concatenate([qt_ref[(g * GROUP_R + r) * HEAD_DIM:(g * GROUP_R + r + 1) * HEAD_DIM, :]
                               for r in range(GROUP_R)], axis=1)
        zero = jnp.zeros_like(top)
        qg = jnp.concatenate([top, zero] if g == 0 else [zero, top], axis=0)
        rows_g = slice(g * HEAD_DIM, (g + 1) * HEAD_DIM)
        unit = lax.broadcasted_iota(jnp.int32, (n_units, lanes), 0)
        p_c = _softmax_cols(_dot(kc_ref[...], qg), unit * CMP_STRIDE + (CMP_LEN - 1) <= qpos)
        o_c = _dot(vct_ref[rows_g, :], p_c.astype(BF16))
        p_sum = p_c[:, 0:Q_BLOCK]
        for r in range(1, GROUP_R):
            p_sum = p_sum + p_c[:, r * Q_BLOCK:(r + 1) * Q_BLOCK]
        sel_ref[...] = _select_blocks(_dot_01(ovl_ref[...], p_sum), qpos1)

        def sel_step(c, carry):
            k0 = pl.multiple_of(c * ck, ck)
            s = _dot(ks_ref[pl.ds(k0, ck), :], qg)
            kpos = k0 + lax.broadcasted_iota(jnp.int32, (ck, lanes), 0)
            mask = _block_mask(sel_ref, c * blocks_per_chunk, blocks_per_chunk, GROUP_R) & (kpos <= qpos)
            return _online_update(carry, s, mask, lambda p: _dot(vst_ref[rows_g, pl.ds(k0, ck)], p))

        n_chunks = (start + Q_BLOCK - 1) // ck + 1
        o_s = _online_finish(lax.fori_loop(0, n_chunks, sel_step, _online_init(HEAD_DIM, lanes)))
        carry = _online_init(HEAD_DIM, lanes)
        for j in range(WIN_BLOCKS):
            wpos = start - WINDOW + j * Q_BLOCK + lax.broadcasted_iota(jnp.int32, (Q_BLOCK, lanes), 0)
            dpos = qpos - wpos
            mask = (dpos >= 0) & (dpos < WINDOW) & (wpos >= 0)
            vwt = vwt_refs[j]
            carry = _online_update(carry, _dot(kw_refs[j][...], qg), mask, lambda p, vwt=vwt: _dot(vwt[rows_g, :], p))
        o_w = _online_finish(carry)
        for r in range(GROUP_R):
            h = g * GROUP_R + r
            ln = slice(r * Q_BLOCK, (r + 1) * Q_BLOCK)
            o = (gt_ref[h:h + 1, :] * o_c[:, ln] + gt_ref[N_HEADS_A + h:N_HEADS_A + h + 1, :] * o_s[:, ln]
                 + gt_ref[2 * N_HEADS_A + h:2 * N_HEADS_A + h + 1, :] * o_w[:, ln])
            o_ref[h * HEAD_DIM:(h + 1) * HEAD_DIM, :] = o.astype(o_ref.dtype)


def _nsa_prompt(qt, kc, vct, ks, vst, kw, vwt, small_t):
    t = qt.shape[1]
    n_units = kc.shape[0]
    n_blocks = t // SEL_BLOCK
    ovl = _overlap_matrix(n_blocks, n_units)
    nq = t // Q_BLOCK
    const = lambda a: pl.BlockSpec(a.shape, lambda i: (0, 0))
    back = WINDOW // Q_BLOCK
    kw_specs = [pl.BlockSpec((Q_BLOCK, KV_HALF), lambda i, j=j: (jnp.maximum(i - back + j, 0), 0))
                for j in range(WIN_BLOCKS)]
    vwt_specs = [pl.BlockSpec((KV_HALF, Q_BLOCK), lambda i, j=j: (0, jnp.maximum(i - back + j, 0)))
                 for j in range(WIN_BLOCKS)]
    return pl.pallas_call(
        _nsa_prompt_kernel,
        grid=(nq,),
        in_specs=[pl.BlockSpec((A_WIDTH, Q_BLOCK), lambda i: (0, i)), const(kc), const(vct), const(ovl),
                  const(ks), const(vst)] + kw_specs + vwt_specs
                 + [pl.BlockSpec((SMALL_COLS, Q_BLOCK), lambda i: (0, i))],
        out_specs=pl.BlockSpec((A_WIDTH, Q_BLOCK), lambda i: (0, i)),
        out_shape=jax.ShapeDtypeStruct((A_WIDTH, t), BF16),
        scratch_shapes=[pltpu.VMEM((n_blocks, Q_BLOCK), F32)],
        compiler_params=_params("arbitrary"),
        name="nsa_prompt",
    )(qt, kc, vct, ovl, ks, vst, *([kw] * WIN_BLOCKS), *([vwt] * WIN_BLOCKS), small_t)


SAMPLE_KV_CHUNK = 512
NEW_ROWS_PAD = 16


def _split3(x):
    hi = x.astype(BF16)
    r1 = x - hi.astype(F32)
    mid = r1.astype(BF16)
    return hi, mid, (r1 - mid.astype(F32)).astype(BF16)


def _nsa_sample_kernel(pt_ref, pool_ref, qbd_ref, kc_ref, vct_ref, ovl_ref, rep_ref, new_ref, win_ref, wnew_ref,
                       gl_ref, o_ref, kbuf, vbuf, sel_ref, sem, *, t_new):
    slot = _gather_pipeline(pt_ref, pool_ref, kbuf, vbuf, sem)
    past = kbuf.shape[1]
    lanes = LANES
    qbd = qbd_ref[0]
    qpos = past + lax.broadcasted_iota(jnp.int32, (1, lanes), 1) % t_new
    n_units = kc_ref.shape[1]
    unit = lax.broadcasted_iota(jnp.int32, (n_units, lanes), 0)
    p_c = _softmax_cols(_dot(kc_ref[0], qbd), unit * CMP_STRIDE + (CMP_LEN - 1) <= qpos)
    o_c = _dot(vct_ref[0], p_c.astype(BF16))
    imp = _dot_01(ovl_ref[...], p_c)
    imp = sum(_dot(t, rep_ref[...]) for t in _split3(imp))
    sel_ref[...] = _select_blocks(imp, qpos)
    ck = SAMPLE_KV_CHUNK
    blocks_per_chunk = ck // SEL_BLOCK

    def sel_step(c, carry):
        k0 = pl.multiple_of(c * ck, ck)
        s = _dot(kbuf[slot, pl.ds(k0, ck), :].astype(BF16), qbd)
        mask = _block_mask(sel_ref, c * blocks_per_chunk, blocks_per_chunk, 1)
        return _online_update(carry, s, mask, lambda p: _dot_tn(vbuf[slot, pl.ds(k0, ck), :].astype(BF16), p))

    carry = lax.fori_loop(0, past // ck, sel_step, _online_init(KV_HALF, lanes))

    def new_rows(ref):
        x = ref[0]
        pad = jnp.zeros((NEW_ROWS_PAD - t_new, KV_COLS), F32)
        x = jnp.concatenate([x, pad], axis=0).astype(BF16)
        return x[:, 0:KV_HALF], x[:, KV_HALF:KV_COLS]

    row = lax.broadcasted_iota(jnp.int32, (NEW_ROWS_PAD, lanes), 0)
    new_ok = (row < t_new) & (past + row <= qpos)
    k_new, v_new = new_rows(new_ref)
    new_sel = jnp.broadcast_to(sel_ref[pl.ds(past // SEL_BLOCK, 1), :], (NEW_ROWS_PAD, lanes)) > 0.5
    carry = _online_update(carry, _dot(k_new, qbd), new_sel & new_ok, lambda p: _dot_tn(v_new, p))
    o_s = _online_finish(carry)
    n_win = win_ref.shape[1]
    wpos = past - n_win + lax.broadcasted_iota(jnp.int32, (n_win, lanes), 0)
    dpos = qpos - wpos
    w_mask = (dpos >= 0) & (dpos < WINDOW) & (wpos >= 0)
    k_w = win_ref[0, :, 0:KV_HALF].astype(BF16)
    carry = _online_update(_online_init(KV_HALF, lanes), _dot(k_w, qbd), w_mask,
                           lambda p: _dot_tn(win_ref[0, :, KV_HALF:KV_COLS].astype(BF16), p))
    k_wn, v_wn = new_rows(wnew_ref)
    carry = _online_update(carry, _dot(k_wn, qbd), new_ok & (qpos - past - row < WINDOW), lambda p: _dot_tn(v_wn, p))
    o_w = _online_finish(carry)
    o_ref[0] = gl_ref[0, 0:1, :] * o_c + gl_ref[0, 1:2, :] * o_s + gl_ref[0, 2:3, :] * o_w


def _nsa_sample(page_table, pool, q, kc, vct, kvs_new, win_state, kvw_new, gates):
    b, n_pages = page_table.shape
    t_new = q.shape[1]
    page_rows = pool.shape[1]
    past = n_pages * page_rows
    n_units = kc.shape[1]
    assert t_new <= SEL_BLOCK and t_new <= NEW_ROWS_PAD and past % SAMPLE_KV_CHUNK == 0
    used = N_KV_A * GROUP_R * t_new
    assert used <= LANES
    qs = q.reshape(b, t_new, N_KV_A, GROUP_R, HEAD_DIM).transpose(0, 2, 4, 3, 1)
    qs = qs.reshape(b, N_KV_A, HEAD_DIM, GROUP_R * t_new)
    qbd = jnp.zeros((b, N_KV_A, HEAD_DIM, N_KV_A, GROUP_R * t_new), BF16)
    for g in range(N_KV_A):
        qbd = qbd.at[:, g, :, g, :].set(qs[:, g])
    qbd = jnp.pad(qbd.reshape(b, KV_HALF, used), ((0, 0), (0, 0), (0, LANES - used)))
    lane = jnp.arange(LANES)
    same = ((lane[:, None] // (GROUP_R * t_new) == lane[None, :] // (GROUP_R * t_new))
            & (lane[:, None] % t_new == lane[None, :] % t_new) & (lane[:, None] < used) & (lane[None, :] < used))
    rep = same.astype(BF16)
    gl = gates.reshape(b, t_new, 3, N_KV_A, GROUP_R).transpose(0, 2, 3, 4, 1).reshape(b, 3, used)
    gl = jnp.pad(gl, ((0, 0), (0, 8 - 3), (0, LANES - used)))
    n_blocks = -(-(past + t_new) // SEL_BLOCK)
    n_blocks_pad = -(-n_blocks // 8) * 8
    ovl = _overlap_matrix(n_blocks_pad, n_units)
    n_slots = min(2, b)
    n_win = win_state.shape[1]
    const = lambda a: pl.BlockSpec(a.shape, lambda i, pt: (0,) * a.ndim)
    per_seq = lambda a: pl.BlockSpec((1,) + a.shape[1:], lambda i, pt: (i,) + (0,) * (a.ndim - 1))
    kvs_new = kvs_new.reshape(b, t_new, KV_COLS)
    kvw_new = kvw_new.reshape(b, t_new, KV_COLS)
    grid_spec = pltpu.PrefetchScalarGridSpec(
        num_scalar_prefetch=1,
        grid=(b,),
        in_specs=[pl.BlockSpec(memory_space=pl.ANY), per_seq(qbd), per_seq(kc), per_seq(vct), const(ovl), const(rep),
                  per_seq(kvs_new), per_seq(win_state), per_seq(kvw_new), per_seq(gl)],
        out_specs=pl.BlockSpec((1, KV_HALF, LANES), lambda i, pt: (i, 0, 0)),
        scratch_shapes=[pltpu.VMEM((n_slots, past, KV_HALF), F32), pltpu.VMEM((n_slots, past, KV_HALF), F32),
                        pltpu.VMEM((n_blocks_pad, LANES), F32), pltpu.SemaphoreType.DMA((n_slots, 2))],
    )
    o = pl.pallas_call(
        functools.partial(_nsa_sample_kernel, t_new=t_new),
        grid_spec=grid_spec,
        out_shape=jax.ShapeDtypeStruct((b, KV_HALF, LANES), F32),
        compiler_params=_params("arbitrary"),
        name="nsa_sample",
    )(page_table, pool, qbd, kc, vct, ovl, rep, kvs_new, win_state, kvw_new, gl)
    o = o[:, :, :used].reshape(b, N_KV_A, HEAD_DIM, N_KV_A, GROUP_R, t_new)
    o = jnp.stack([o[:, g, :, g] for g in range(N_KV_A)], axis=1)
    return o.transpose(0, 4, 1, 3, 2).reshape(b * t_new, A_WIDTH)


CONV_PAD = 8


def _log_sigmoid(x):
    return jnp.minimum(x, 0.0) - jnp.log(1.0 + jnp.exp(-jnp.abs(x)))


def _mlstm_kernel(u_ref, vm_ref, om_ref, sm_ref, ift_ref, cw_ref, cb_ref, wq_ref, wk_ref, mg_ref,
                  c0_ref, n0_ref, m0_ref, conv0_ref, hm_ref, c_out, n_out, m_out, conv_out,
                  c_s, n_s, m_s, ext_s):
    c = pl.program_id(1)
    chunk = u_ref.shape[0]
    keep = CONV_W - 1

    @pl.when(c == 0)
    def _():
        c_s[...] = c0_ref[0]
        n_s[...] = n0_ref[0]
        m_s[...] = m0_ref[0]
        ext_s[0:CONV_PAD, :] = conv0_ref[0]

    ext_s[CONV_PAD:CONV_PAD + chunk, :] = u_ref[...]
    conv = cb_ref[...]
    for j in range(CONV_W):
        conv = conv + ext_s[pl.ds(CONV_PAD - keep + j, chunk), :] * cw_ref[j:j + 1, :]
    ext_s[CONV_PAD - keep:CONV_PAD, :] = ext_s[CONV_PAD + chunk - keep:CONV_PAD + chunk, :]
    xc = (conv * jax.nn.sigmoid(conv)).astype(BF16)
    ti = lax.broadcasted_iota(jnp.int32, (chunk, chunk), 0)
    si = lax.broadcasted_iota(jnp.int32, (chunk, chunk), 1)
    causal = si <= ti
    head_lane = lax.broadcasted_iota(jnp.int32, (1, LANES), 1)
    sm = sm_ref[...]
    m_all = m_s[...]
    for h in range(N_HEADS_M):
        cols = slice(h * HEAD_DIM_M, (h + 1) * HEAD_DIM_M)
        xh = xc[:, cols]
        qm = _dot(xh, wq_ref[h])
        km = _dot(xh, wk_ref[h]) * HEAD_DIM_M ** -0.5
        qb, kb = qm.astype(BF16), km.astype(BF16)
        vb = vm_ref[:, cols].astype(BF16)
        ig_col = sm[:, SM_IG + h:SM_IG + h + 1]
        lf_col = _log_sigmoid(sm[:, SM_FG + h:SM_FG + h + 1])
        ig_row = ift_ref[0, 0, h:h + 1, :]
        lf_row = _log_sigmoid(ift_ref[0, 0, N_HEADS_M + h:N_HEADS_M + h + 1, :])
        fcum_col = jnp.sum(jnp.where(causal, lf_row, 0.0), axis=1, keepdims=True)
        fcum_row = jnp.sum(jnp.where(ti <= si, lf_col, 0.0), axis=0, keepdims=True)
        d_row = ig_row - fcum_row
        cmax_col = jnp.max(jnp.where(causal, d_row, NEG), axis=1, keepdims=True)
        m_prev = m_all[:, h:h + 1]
        m_row = fcum_col + jnp.maximum(m_prev, cmax_col)
        w_intra = jnp.exp(jnp.where(causal, (fcum_col - m_row) + d_row, NEG))
        w_inter = jnp.exp(m_prev + fcum_col - m_row)
        c_prev = c_s[h]
        n_prev = n_s[h:h + 1, :]
        qk = _dot_nt(qb, kb) * w_intra
        num = w_inter * _dot(qb, c_prev.astype(BF16)) + _dot(qk.astype(BF16), vb)
        den = w_inter * jnp.sum(qm * n_prev, axis=1, keepdims=True) + jnp.sum(qk, axis=1, keepdims=True)
        hh = num / jnp.maximum(jnp.abs(den), jnp.exp(-m_row))
        m_new = m_row[chunk - 1:chunk, :]
        f_last = fcum_col[chunk - 1:chunk, :]
        w_keep = jnp.exp(m_prev + f_last - m_new)
        w_src = jnp.exp(ig_col + f_last - fcum_col - m_new)
        ks = km * w_src
        c_s[h] = w_keep * c_prev + _dot_tn(ks.astype(BF16), vb)
        n_s[h:h + 1, :] = w_keep * n_prev + jnp.sum(ks, axis=0, keepdims=True)
        m_all = jnp.where(head_lane == h, m_new, m_all)
        y = hh * lax.rsqrt(jnp.mean(hh * hh, axis=1, keepdims=True) + EPS) * mg_ref[:, cols]
        hm_ref[:, cols] = y * jax.nn.sigmoid(om_ref[:, cols])
    m_s[...] = m_all

    @pl.when(c == pl.num_programs(1) - 1)
    def _():
        c_out[0] = c_s[...]
        n_out[0] = n_s[...]
        m_out[0] = m_s[...]
        conv_out[0] = ext_s[0:CONV_PAD, :]


def _mlstm(u, vm, om, small, conv_w, conv_b, w_mq, w_mk, mnorm_g, c0, n0, m0, conv0, b, t, chunk):
    n_chunks = t // chunk
    ift = small[:, SM_IG:SM_IG + 2 * N_HEADS_M].reshape(b, n_chunks, chunk, 2 * N_HEADS_M).transpose(0, 1, 3, 2)
    m0p = jnp.pad(m0.reshape(b, 1, N_HEADS_M), ((0, 0), (0, 0), (0, LANES - N_HEADS_M)))
    conv0p = jnp.pad(conv0, ((0, 0), (CONV_PAD - (CONV_W - 1), 0), (0, 0)))
    row = lambda w: pl.BlockSpec((chunk, w), lambda i, c: (i * n_chunks + c, 0))
    const = lambda a: pl.BlockSpec(a.shape, lambda i, c: (0,) * a.ndim)
    per_seq = lambda a: pl.BlockSpec((1,) + a.shape[1:], lambda i, c: (i,) + (0,) * (a.ndim - 1))
    wq = w_mq.astype(BF16)
    wk = w_mk.astype(BF16)
    cb = conv_b.reshape(1, M_WIDTH)
    mg = mnorm_g.reshape(1, M_WIDTH)
    out_shape = (jax.ShapeDtypeStruct((b * t, M_WIDTH), F32),
                 jax.ShapeDtypeStruct(c0.shape, F32), jax.ShapeDtypeStruct(n0.shape, F32),
                 jax.ShapeDtypeStruct(m0p.shape, F32), jax.ShapeDtypeStruct(conv0p.shape, F32))
    hm, c_new, n_new, m_new, conv_new = pl.pallas_call(
        _mlstm_kernel,
        grid=(b, n_chunks),
        in_specs=[row(M_WIDTH), row(M_WIDTH), row(M_WIDTH), row(SMALL_COLS),
                  pl.BlockSpec((1, 1, 2 * N_HEADS_M, chunk), lambda i, c: (i, c, 0, 0)),
                  const(conv_w), const(cb), const(wq), const(wk), const(mg),
                  per_seq(c0), per_seq(n0), per_seq(m0p), per_seq(conv0p)],
        out_specs=(row(M_WIDTH), per_seq(c0), per_seq(n0), per_seq(m0p), per_seq(conv0p)),
        out_shape=out_shape,
        scratch_shapes=[pltpu.VMEM(c0.shape[1:], F32), pltpu.VMEM(n0.shape[1:], F32), pltpu.VMEM((1, LANES), F32),
                        pltpu.VMEM((CONV_PAD + chunk, M_WIDTH), F32)],
        compiler_params=_params("arbitrary", "arbitrary"),
        name="mlstm",
    )(u, vm, om, small, ift, conv_w, cb, wq, wk, mg, c0, n0, m0p, conv0p)
    return hm, c_new, n_new, m_new[:, 0, :N_HEADS_M], conv_new[:, CONV_PAD - (CONV_W - 1):, :]


def _outproj_kernel(x_ref, oa_ref, hm_ref, wo_ref, g1_ref, ng_ref, sc_ref, sh_ref, wq_ref,
                    x1_ref, h2_ref, pq_ref, *, oa_transposed):
    oa = oa_ref[...].astype(BF16)
    wo_a = wo_ref[0:A_WIDTH, :]
    mix = _dot_tn(oa, wo_a) if oa_transposed else _dot(oa, wo_a)
    mix = mix + _dot(hm_ref[...].astype(BF16), wo_ref[A_WIDTH:A_WIDTH + M_WIDTH, :])
    x1 = x_ref[...] + g1_ref[...] * mix
    x1_ref[...] = x1
    y = x1 * lax.rsqrt(jnp.mean(x1 * x1, axis=-1, keepdims=True) + EPS) * ng_ref[...]
    h2 = (y * (1.0 + sc_ref[...]) + sh_ref[...]).astype(BF16)
    h2_ref[...] = h2
    pq_ref[...] = _dot(h2, wq_ref[...]).astype(BF16)


def _outproj(x, oa, hm, w_out, g1, norm_g, sc, sh, peer_wq, tm, oa_transposed):
    r, d = x.shape
    per_row = g1.shape[0] != 1
    mod = pl.BlockSpec((tm, d), lambda i: (i, 0)) if per_row else pl.BlockSpec((1, d), lambda i: (0, 0))
    row = lambda w: pl.BlockSpec((tm, w), lambda i: (i, 0))
    const = lambda a: pl.BlockSpec(a.shape, lambda i: (0, 0))
    oa_spec = pl.BlockSpec((A_WIDTH, tm), lambda i: (0, i)) if oa_transposed else row(A_WIDTH)
    nq = peer_wq.shape[1]
    return pl.pallas_call(
        functools.partial(_outproj_kernel, oa_transposed=oa_transposed),
        grid=(r // tm,),
        in_specs=[row(d), oa_spec, row(M_WIDTH), const(w_out), mod, const(norm_g), mod, mod, const(peer_wq)],
        out_specs=(row(d), row(d), row(nq)),
        out_shape=(jax.ShapeDtypeStruct((r, d), F32), jax.ShapeDtypeStruct((r, d), BF16),
                   jax.ShapeDtypeStruct((r, nq), BF16)),
        compiler_params=_params("arbitrary"),
        name="outproj",
    )(x, oa, hm, w_out, g1, norm_g, sc, sh, peer_wq)


PEER_CHUNK_I = 8
PEER_HALF = PEER_DKEY // 2


def _top_values(work, n):
    rows = lax.broadcasted_iota(jnp.int32, work.shape, 0)
    vals = []
    for _ in range(n):
        mx = jnp.max(work, axis=0, keepdims=True)
        first = jnp.min(jnp.where(work == mx, rows, work.shape[0]), axis=0, keepdims=True)
        work = jnp.where(rows == first, REMOVED, work)
        vals.append(mx)
    return jnp.concatenate(vals, axis=0)


def _peer_kernel(pq_ref, keys_ref, h2t_ref, u_ref, vt_ref, x1_ref, g2_ref, ng_ref, y_ref,
                 s_s, top_s, td_s, e1_s, e2_s, act_s, w_s, acc_s):
    j = pl.program_id(1)
    tt = pq_ref.shape[0]
    k = PEER_TOPK

    @pl.when(j == 0)
    def _():
        def score(hc, _):
            col = pl.multiple_of(hc * PEER_HALF, PEER_HALF)
            s = _dot_nt(keys_ref[hc], pq_ref[:, pl.ds(col, PEER_HALF)])
            s_s[hc] = s
            top_s[hc] = _top_values(s, k)
            return 0
        lax.fori_loop(0, 2 * PEER_HEADS, score, 0)

        def head(h, _):
            a = top_s[2 * h]
            b = top_s[2 * h + 1]
            pieces = [a[0:1] + b]
            pieces += [a[i:i + 1] + b[0:k // 2] for i in range(1, k // 2)]
            pieces += [a[k // 2:k] + b[0:1]]
            tv = _top_values(jnp.concatenate(pieces, axis=0), k)
            thr = tv[k - 1:k]
            z = jnp.sum(jnp.exp(tv - tv[0:1]), axis=0, keepdims=True)
            td_s[h] = thr - s_s[2 * h]
            e1_s[h] = jnp.exp(s_s[2 * h] - a[0:1])
            e2_s[h] = jnp.exp(s_s[2 * h + 1] - b[0:1]) / z
            return 0
        lax.fori_loop(0, PEER_HEADS, head, 0)
        acc_s[...] = jnp.zeros(acc_s.shape, F32)

    act_s[...] = jax.nn.gelu(_dot(u_ref[...], h2t_ref[...]), approximate=True)

    def build(ii, _):
        i = j * PEER_CHUNK_I + ii
        g = jnp.zeros((PEER_NKEYS, tt), F32)
        for h in range(PEER_HEADS):
            td = td_s[h, pl.ds(i, 1), :]
            e1 = e1_s[h, pl.ds(i, 1), :]
            g = g + jnp.where(s_s[2 * h + 1] >= td, e2_s[h] * e1, 0.0)
        rows = pl.ds(pl.multiple_of(ii * PEER_NKEYS, PEER_NKEYS), PEER_NKEYS)
        w_s[rows, :] = (g * act_s[rows, :]).astype(BF16)
        return 0
    lax.fori_loop(0, PEER_CHUNK_I, build, 0)
    acc_s[...] += _dot(vt_ref[...], w_s[...])

    @pl.when(j == pl.num_programs(1) - 1)
    def _():
        x2 = x1_ref[...] + g2_ref[...] * acc_s[...].T
        y_ref[...] = x2 * lax.rsqrt(jnp.mean(x2 * x2, axis=-1, keepdims=True) + EPS) * ng_ref[...]


def _peer(pq, keys, h2t, u_tab, vt_tab, x1, g2, norm_g, tt):
    n_tok, d = x1.shape
    n_exp = u_tab.shape[0]
    ec = PEER_CHUNK_I * PEER_NKEYS
    per_row = g2.shape[0] != 1
    mod = pl.BlockSpec((tt, d), lambda t, j: (t, 0)) if per_row else pl.BlockSpec((1, d), lambda t, j: (0, 0))
    f32_tok = lambda n: pltpu.VMEM((n, PEER_NKEYS, tt), F32)
    return pl.pallas_call(
        _peer_kernel,
        grid=(n_tok // tt, n_exp // ec),
        in_specs=[pl.BlockSpec((tt, pq.shape[1]), lambda t, j: (t, 0)),
                  pl.BlockSpec(keys.shape, lambda t, j: (0, 0, 0)),
                  pl.BlockSpec((d, tt), lambda t, j: (0, t)),
                  pl.BlockSpec((ec, d), lambda t, j: (j, 0)),
                  pl.BlockSpec((d, ec), lambda t, j: (0, j)),
                  pl.BlockSpec((tt, d), lambda t, j: (t, 0)), mod,
                  pl.BlockSpec((1, d), lambda t, j: (0, 0))],
        out_specs=pl.BlockSpec((tt, d), lambda t, j: (t, 0)),
        out_shape=jax.ShapeDtypeStruct((n_tok, d), F32),
        scratch_shapes=[f32_tok(2 * PEER_HEADS), pltpu.VMEM((2 * PEER_HEADS, PEER_TOPK, tt), F32),
                        f32_tok(PEER_HEADS), f32_tok(PEER_HEADS), f32_tok(PEER_HEADS),
                        pltpu.VMEM((ec, tt), F32), pltpu.VMEM((ec, tt), BF16), pltpu.VMEM((d, tt), F32)],
        compiler_params=_params("arbitrary", "arbitrary"),
        name="peer",
    )(pq, keys, h2t, u_tab, vt_tab, x1, g2, norm_g)


def kernel(x_prompt, x_sample, cache_cmp_kv, cache_slc_kv, page_table, state_win_kv, state_C, state_n, state_m, state_conv, c_prompt, c_sample, w_ada, b_ada, norm1_g, w_in, b_gate_a, cmp_pe_k, cmp_w1_k, cmp_w2_k, cmp_pe_v, cmp_w1_v, cmp_w2_v, conv_w, conv_b, w_mq, w_mk, b_i, b_f, mnorm_g, w_out, norm2_g, peer_wq, peer_keys, peer_u, peer_v, normf_g):
    b_p, t_p, dm = x_prompt.shape
    b_s, t_s, _ = x_sample.shape
    page_rows = cache_cmp_kv.shape[1]
    past_len = page_table.shape[1] * page_rows
    kv_shape = (2, N_KV_A, HEAD_DIM)

    c_all = jnp.concatenate([c_prompt, c_sample], axis=0)
    ada = _ada(jnp.pad(c_all, ((0, -c_all.shape[0] % 8), (0, 0))), w_ada, b_ada)
    w_in_p, bias_small = _permute_w_in(w_in, b_gate_a, b_i, b_f)
    cw = _compress_weights(cmp_pe_k, cmp_w1_k, cmp_w2_k, cmp_pe_v, cmp_w1_v, cmp_w2_v)
    w_out_b = w_out.astype(BF16)
    wq_b = peer_wq.astype(BF16)
    keys_b = peer_keys.reshape(2 * PEER_HEADS, PEER_NKEYS, PEER_HALF).astype(BF16)
    u_b = peer_u.astype(BF16)
    vt_b = peer_v.T.astype(BF16)
    row_vec = lambda v: v.reshape(1, -1)

    def modulation(rows, t):
        parts = [rows[:, k * dm:(k + 1) * dm] for k in range(6)]
        return parts if rows.shape[0] == 1 else [jnp.repeat(p, t, axis=0) for p in parts]

    def tile_rows(n):
        return min(256, n)

    def peer_stage(pq, h2, x1, g2):
        n = x1.shape[0]
        tt = min(512, -(-n // LANES) * LANES)
        pad = -n % tt
        padr = lambda a: jnp.pad(a, ((0, pad), (0, 0)))
        g2p = g2 if g2.shape[0] == 1 else padr(g2)
        y = _peer(padr(pq), keys_b, padr(h2).T, u_b, vt_b, padr(x1), g2p, row_vec(normf_g), tt)
        return y[:n]

    sh1, sc1, g1, sh2, sc2, g2 = modulation(ada[0:b_p], t_p)
    xp = x_prompt.reshape(b_p * t_p, dm)
    pos_p = jnp.tile(jnp.arange(t_p), b_p)
    q, kvc_p, kvs_p, kvw_p, u, vm, om, sm = _inproj(xp, row_vec(norm1_g), sc1, sh1, w_in_p, bias_small,
                                                    _rot_tables(pos_p), tile_rows(b_p * t_p))
    table = jnp.arange(t_p // page_rows, dtype=jnp.int32).reshape(1, -1)
    sm_t = sm.T
    oa_t = []
    for s in range(b_p):
        rows = slice(s * t_p, (s + 1) * t_p)
        kc, vct = _compress(table, kvc_p[rows].reshape(t_p // page_rows, page_rows, KV_COLS), cw)
        ks, kw = kvs_p[rows], kvw_p[rows]
        oa_t.append(_nsa_prompt(q[rows].T, kc[0], vct[0], ks[:, :KV_HALF].astype(BF16),
                                ks[:, KV_HALF:].astype(BF16).T, kw[:, :KV_HALF].astype(BF16),
                                kw[:, KV_HALF:].astype(BF16).T, sm_t[:, rows]))
    oa_t = oa_t[0] if b_p == 1 else jnp.concatenate(oa_t, axis=1)
    hm, c_p, n_p, m_p, conv_p = _mlstm(
        u, vm, om, sm, conv_w, conv_b, w_mq, w_mk, mnorm_g,
        jnp.zeros((b_p, N_HEADS_M, HEAD_DIM_M, HEAD_DIM_M), F32), jnp.zeros((b_p, N_HEADS_M, HEAD_DIM_M), F32),
        jnp.zeros((b_p, N_HEADS_M), F32), jnp.zeros((b_p, CONV_W - 1, M_WIDTH), F32), b_p, t_p, min(128, t_p))
    x1, h2, pq = _outproj(xp, oa_t, hm, w_out_b, g1, row_vec(norm2_g), sc2, sh2, wq_b, tile_rows(b_p * t_p), True)
    y_p = peer_stage(pq, h2, x1, g2).reshape(b_p, t_p, dm)

    sh1, sc1, g1, sh2, sc2, g2 = modulation(ada[b_p:b_p + b_s], t_s)
    xs = x_sample.reshape(b_s * t_s, dm)
    pos_s = jnp.tile(past_len + jnp.arange(t_s), b_s)
    q, kvc_s, kvs_s, kvw_s, u, vm, om, sm = _inproj(xs, row_vec(norm1_g), sc1, sh1, w_in_p, bias_small,
                                                    _rot_tables(pos_s), tile_rows(b_s * t_s))
    n_pool = cache_cmp_kv.shape[0]
    kc, vct = _compress(page_table, cache_cmp_kv.reshape(n_pool, page_rows, KV_COLS), cw)
    n_win = state_win_kv.shape[1]
    oa = _nsa_sample(page_table, cache_slc_kv.reshape(n_pool, page_rows, KV_COLS), q.reshape(b_s, t_s, A_WIDTH),
                     kc, vct, kvs_s, state_win_kv.reshape(b_s, n_win, KV_COLS), kvw_s,
                     sm[:, 0:GATE_COLS].reshape(b_s, t_s, GATE_COLS))
    hm, c_s, n_s, m_s, conv_s = _mlstm(u, vm, om, sm, conv_w, conv_b, w_mq, w_mk, mnorm_g, state_C.astype(F32),
                                       state_n.astype(F32), state_m.astype(F32), state_conv.astype(F32),
                                       b_s, t_s, t_s)
    x1, h2, pq = _outproj(xs, oa, hm, w_out_b, g1, row_vec(norm2_g), sc2, sh2, wq_b, tile_rows(b_s * t_s), False)
    y_s = peer_stage(pq, h2, x1, g2).reshape(b_s, t_s, dm)

    kv5 = lambda a, b, t: a.reshape((b, t) + kv_shape)
    kvw_p5, kvw_s5 = kv5(kvw_p, b_p, t_p), kv5(kvw_s, b_s, t_s)
    win_p = jnp.concatenate([jnp.zeros((b_p, n_win) + kv_shape, F32), kvw_p5], axis=1)[:, -n_win:]
    win_s = jnp.concatenate([state_win_kv.astype(F32), kvw_s5], axis=1)[:, -n_win:]
    return (y_p, y_s, kv5(kvc_p, b_p, t_p), kv5(kvc_s, b_s, t_s), kv5(kvs_p, b_p, t_p), kv5(kvs_s, b_s, t_s),
            win_p, win_s, c_p, c_s, n_p, n_s, m_p, m_s, conv_p, conv_s)
```

```python
import functools

import jax
import jax.numpy as jnp
from jax import lax
from jax.experimental import pallas as pl
from jax.experimental.pallas import tpu as pltpu

F32 = jnp.float32
BF16 = jnp.bfloat16

N_HEADS_A = 8
N_KV_A = 2
HEAD_DIM = 64
GROUP_R = N_HEADS_A // N_KV_A
ROT_DIM = HEAD_DIM // 4
ROPE_THETA = 500000.0
CMP_LEN = 32
CMP_STRIDE = 16
CMP_HID = 256
SEL_BLOCK = 64
N_SEL = 16
WINDOW = 512
Q_BLOCK = 128
FORCE_BONUS = 1.0e3
N_HEADS_M = 4
HEAD_DIM_M = 128
CONV_W = 4
PEER_HEADS = 8
PEER_NKEYS = 128
PEER_TOPK = 16
PEER_DKEY = 256
EPS = 1e-6
NEG = -1e30

A_WIDTH = N_HEADS_A * HEAD_DIM
M_WIDTH = N_HEADS_M * HEAD_DIM_M
KV_COLS = 2 * N_KV_A * HEAD_DIM
GATE_COLS = 3 * N_HEADS_A
KV_HALF = N_KV_A * HEAD_DIM
LANES = 128
SMALL_COLS = LANES
VMEM_LIMIT = 56 * 1024 * 1024


def _dot(a, b):
    return jnp.dot(a, b, preferred_element_type=F32)


def _dot_nt(a, b):
    return lax.dot_general(a, b, (((1,), (1,)), ((), ())), preferred_element_type=F32)


def _dot_tn(a, b):
    return lax.dot_general(a, b, (((0,), (0,)), ((), ())), preferred_element_type=F32)


def _params(*sem):
    return pltpu.CompilerParams(dimension_semantics=sem, vmem_limit_bytes=VMEM_LIMIT)


def _ada_kernel(c_ref, w_ref, b_ref, o_ref):
    c = c_ref[...]
    s = (c * jax.nn.sigmoid(c)).astype(BF16)
    o_ref[...] = _dot(s, w_ref[...].astype(BF16)) + b_ref[...]


def _ada(c, w_ada, b_ada):
    m, d = c.shape
    n = w_ada.shape[1]
    tn = n // 6
    return pl.pallas_call(
        _ada_kernel,
        grid=(n // tn,),
        in_specs=[pl.BlockSpec((m, d), lambda j: (0, 0)),
                  pl.BlockSpec((d, tn), lambda j: (0, j)),
                  pl.BlockSpec((1, tn), lambda j: (0, j))],
        out_specs=pl.BlockSpec((m, tn), lambda j: (0, j)),
        out_shape=jax.ShapeDtypeStruct((m, n), F32),
        compiler_params=_params("arbitrary"),
        name="ada",
    )(c, w_ada, b_ada.reshape(1, n))


IN_OFF_Q = 0
IN_OFF_KVC = A_WIDTH
IN_OFF_KVS = IN_OFF_KVC + KV_COLS
IN_OFF_KVW = IN_OFF_KVS + KV_COLS
IN_OFF_U = IN_OFF_KVW + KV_COLS
IN_OFF_VM = IN_OFF_U + M_WIDTH
IN_OFF_OM = IN_OFF_VM + M_WIDTH
IN_OFF_SMALL = IN_OFF_OM + M_WIDTH
IN_COLS_PADDED = IN_OFF_SMALL + SMALL_COLS
SM_IG = GATE_COLS
SM_FG = GATE_COLS + N_HEADS_M


def _permute_w_in(w_in, b_gate_a, b_i, b_f):
    d = w_in.shape[0]
    o = 0
    parts = {}
    for name, width in (("q", A_WIDTH), ("kvc", KV_COLS), ("kvs", KV_COLS), ("kvw", KV_COLS), ("g", GATE_COLS),
                        ("u", M_WIDTH), ("vm", M_WIDTH), ("om", M_WIDTH), ("i", N_HEADS_M), ("f", N_HEADS_M)):
        parts[name] = w_in[:, o:o + width]
        o += width
    pad = jnp.zeros((d, SMALL_COLS - GATE_COLS - 2 * N_HEADS_M), w_in.dtype)
    w = jnp.concatenate([parts[k] for k in ("q", "kvc", "kvs", "kvw", "u", "vm", "om", "g", "i", "f")] + [pad], axis=1)
    bias = jnp.concatenate([b_gate_a, b_i, b_f, jnp.zeros((SMALL_COLS - GATE_COLS - 2 * N_HEADS_M,), F32)])
    return w.astype(BF16), bias.reshape(1, SMALL_COLS)


def _rot_tables(pos):
    t = pos.shape[0]
    inv = ROPE_THETA ** (-jnp.arange(0, ROT_DIM, 2, dtype=F32) / ROT_DIM)
    ang = pos.astype(F32)[:, None] * inv[None, :]
    cos, sin = jnp.cos(ang), jnp.sin(ang)
    half = ROT_DIM // 2
    ones = jnp.ones((t, HEAD_DIM - ROT_DIM), F32)
    zeros = jnp.zeros((t, HEAD_DIM - ROT_DIM), F32)
    zh = jnp.zeros((t, half), F32)
    c = jnp.concatenate([cos, cos, ones], axis=1)
    s_lo = jnp.concatenate([-sin, zh, zeros], axis=1)
    s_hi = jnp.concatenate([zh, sin, zeros], axis=1)
    rep = LANES // HEAD_DIM
    return jnp.tile(c, (1, rep)), jnp.tile(s_lo, (1, rep)), jnp.tile(s_hi, (1, rep))


def _inproj_kernel(x_ref, g_ref, sc_ref, sh_ref, w_ref, bsm_ref, rc_ref, rlo_ref, rhi_ref,
                   q_ref, kvc_ref, kvs_ref, kvw_ref, u_ref, vm_ref, om_ref, sm_ref):
    x = x_ref[...]
    y = x * lax.rsqrt(jnp.mean(x * x, axis=-1, keepdims=True) + EPS) * g_ref[...]
    hb = (y * (1.0 + sc_ref[...]) + sh_ref[...]).astype(BF16)
    rc, rlo, rhi = rc_ref[...], rlo_ref[...], rhi_ref[...]
    half = ROT_DIM // 2

    def rot(z):
        return z * rc + pltpu.roll(z, LANES - half, 1) * rlo + pltpu.roll(z, half, 1) * rhi

    zq = _dot(hb, w_ref[:, IN_OFF_Q:IN_OFF_Q + A_WIDTH])
    scale = HEAD_DIM ** -0.5
    for c in range(A_WIDTH // LANES):
        q_ref[:, c * LANES:(c + 1) * LANES] = (rot(zq[:, c * LANES:(c + 1) * LANES]) * scale).astype(BF16)
    for ref, off in ((kvc_ref, IN_OFF_KVC), (kvs_ref, IN_OFF_KVS), (kvw_ref, IN_OFF_KVW)):
        z = _dot(hb, w_ref[:, off:off + KV_COLS])
        ref[:, 0:KV_HALF] = rot(z[:, 0:KV_HALF])
        ref[:, KV_HALF:KV_COLS] = z[:, KV_HALF:KV_COLS]
    u_ref[...] = _dot(hb, w_ref[:, IN_OFF_U:IN_OFF_U + M_WIDTH])
    vm_ref[...] = _dot(hb, w_ref[:, IN_OFF_VM:IN_OFF_VM + M_WIDTH])
    om_ref[...] = _dot(hb, w_ref[:, IN_OFF_OM:IN_OFF_OM + M_WIDTH])
    zs = _dot(hb, w_ref[:, IN_OFF_SMALL:IN_OFF_SMALL + SMALL_COLS]) + bsm_ref[...]
    lane = lax.broadcasted_iota(jnp.int32, zs.shape, 1)
    sm_ref[...] = jnp.where(lane < GATE_COLS, jax.nn.sigmoid(zs), zs)


def _inproj(x, norm_g, sc, sh, w_in_p, bias_small, rot_tabs, tm):
    r, d = x.shape
    per_row = sc.shape[0] != 1
    mod_spec = pl.BlockSpec((tm, d), lambda i: (i, 0)) if per_row else pl.BlockSpec((1, d), lambda i: (0, 0))
    row = lambda w: pl.BlockSpec((tm, w), lambda i: (i, 0))
    const = lambda a: pl.BlockSpec(a.shape, lambda i: (0, 0))
    out_shape = (jax.ShapeDtypeStruct((r, A_WIDTH), BF16),
                 jax.ShapeDtypeStruct((r, KV_COLS), F32), jax.ShapeDtypeStruct((r, KV_COLS), F32),
                 jax.ShapeDtypeStruct((r, KV_COLS), F32),
                 jax.ShapeDtypeStruct((r, M_WIDTH), F32), jax.ShapeDtypeStruct((r, M_WIDTH), F32),
                 jax.ShapeDtypeStruct((r, M_WIDTH), F32), jax.ShapeDtypeStruct((r, SMALL_COLS), F32))
    return pl.pallas_call(
        _inproj_kernel,
        grid=(r // tm,),
        in_specs=[row(d), const(norm_g), mod_spec, mod_spec, const(w_in_p), const(bias_small),
                  row(LANES), row(LANES), row(LANES)],
        out_specs=(row(A_WIDTH), row(KV_COLS), row(KV_COLS), row(KV_COLS), row(M_WIDTH), row(M_WIDTH),
                   row(M_WIDTH), row(SMALL_COLS)),
        out_shape=out_shape,
        compiler_params=_params("arbitrary"),
        name="inproj",
    )(x, norm_g, sc, sh, w_in_p, bias_small, *rot_tabs)


def _transposed_pool(rows5):
    n, r = rows5.shape[0], rows5.shape[1]
    return rows5.transpose(0, 2, 3, 4, 1).reshape(n, KV_COLS, r)


def _page_copy(pt_ref, pool_ref, xt, sem, seq, slot, p):
    page_rows = pool_ref.shape[2]
    dst = pl.ds(pl.multiple_of(p * page_rows, page_rows), page_rows)
    return pltpu.make_async_copy(pool_ref.at[pt_ref[seq, p]], xt.at[slot, :, dst], sem.at[slot])


def _gather_start(pt_ref, pool_ref, xt, sem, seq, slot):
    def body(p, _):
        _page_copy(pt_ref, pool_ref, xt, sem, seq, slot, p).start()
        return 0
    lax.fori_loop(0, pt_ref.shape[1], body, 0)


def _gather_wait(pt_ref, pool_ref, xt, sem, seq, slot):
    def body(p, _):
        _page_copy(pt_ref, pool_ref, xt, sem, seq, slot, p).wait()
        return 0
    lax.fori_loop(0, pt_ref.shape[1], body, 0)


def _gather_pipeline(pt_ref, pool_ref, xt, sem):
    b = pl.program_id(0)
    nb = pl.num_programs(0)
    n_slots = xt.shape[0]
    slot = b % n_slots

    @pl.when(b == 0)
    def _():
        _gather_start(pt_ref, pool_ref, xt, sem, 0, 0)

    if n_slots > 1:
        @pl.when(b + 1 < nb)
        def _():
            _gather_start(pt_ref, pool_ref, xt, sem, b + 1, 1 - slot)

    _gather_wait(pt_ref, pool_ref, xt, sem, b, slot)
    return slot


CMP_UNIT = CMP_STRIDE
CMP_SLAB = 256


def _compress_weights(pe_k, w1_k, w2_k, pe_v, w1_v, w2_v):
    eye = jnp.eye(N_KV_A, dtype=F32)

    def split_w1(w1):
        w = w1.reshape(CMP_LEN, HEAD_DIM, CMP_HID)

        def bd(part):
            return jnp.einsum('ldh,gk->lgdkh', part, eye).reshape(CMP_UNIT * KV_HALF, N_KV_A * CMP_HID).astype(BF16)
        return bd(w[:CMP_UNIT]), bd(w[CMP_UNIT:])

    def bd_w2(w2):
        return jnp.einsum('hd,gk->ghkd', w2, eye).reshape(N_KV_A * CMP_HID, KV_HALF)

    wka, wkb = split_w1(w1_k)
    wva, wvb = split_w1(w1_v)
    pe_a = jnp.concatenate([jnp.tile(pe_k[:CMP_UNIT], (1, N_KV_A)), jnp.tile(pe_v[:CMP_UNIT], (1, N_KV_A))], axis=1)
    pe_b = jnp.concatenate([jnp.tile(pe_k[CMP_UNIT:], (1, N_KV_A)), jnp.tile(pe_v[CMP_UNIT:], (1, N_KV_A))], axis=1)
    return (pe_a, pe_b, wka, wkb, wva, wvb, bd_w2(w2_k).astype(BF16), bd_w2(w2_v).T.astype(BF16))


def _compress_kernel(pt_ref, pool_ref, pea_ref, peb_ref, wka_ref, wkb_ref, wva_ref, wvb_ref, w2k_ref, w2vt_ref,
                     kc_ref, vct_ref, xt, kbuf, vbuf, a_k, b_k, a_v, b_v, sem):
    slot = _gather_pipeline(pt_ref, pool_ref, xt, sem)
    n_units = kc_ref.shape[1]
    slab = min(CMP_SLAB, n_units)
    page_rows = pool_ref.shape[2]
    for s in range(n_units // slab):
        base = s * slab * CMP_UNIT
        for p in range(slab * CMP_UNIT // page_rows):
            x = xt[slot, :, base + p * page_rows:base + (p + 1) * page_rows]
            dst = pl.ds(p * page_rows, page_rows)
            kbuf[dst, :] = x[0:KV_HALF, :].T
            vbuf[dst, :] = x[KV_HALF:KV_COLS, :].T
        zka, zkb, zva, zvb = [], [], [], []
        for l in range(CMP_UNIT):
            xk = kbuf[pl.ds(l, slab, stride=CMP_UNIT), :]
            xv = vbuf[pl.ds(l, slab, stride=CMP_UNIT), :]
            zka.append((xk + pea_ref[l:l + 1, 0:KV_HALF]).astype(BF16))
            zkb.append((xk + peb_ref[l:l + 1, 0:KV_HALF]).astype(BF16))
            zva.append((xv + pea_ref[l:l + 1, KV_HALF:KV_COLS]).astype(BF16))
            zvb.append((xv + peb_ref[l:l + 1, KV_HALF:KV_COLS]).astype(BF16))
        rows = pl.ds(s * slab, slab)
        a_k[rows, :] = _dot(jnp.concatenate(zka, axis=1), wka_ref[...])
        b_k[rows, :] = _dot(jnp.concatenate(zkb, axis=1), wkb_ref[...])
        a_v[rows, :] = _dot(jnp.concatenate(zva, axis=1), wva_ref[...])
        b_v[rows, :] = _dot(jnp.concatenate(zvb, axis=1), wvb_ref[...])
    tail = pl.ds(n_units, 8)
    b_k[tail, :] = jnp.zeros((8, b_k.shape[1]), F32)
    b_v[tail, :] = jnp.zeros((8, b_v.shape[1]), F32)
    for s in range(n_units // slab):
        rows = pl.ds(s * slab, slab)
        nxt = pl.ds(s * slab + 1, slab)
        hid_k = jax.nn.gelu(a_k[rows, :] + b_k[nxt, :], approximate=True).astype(BF16)
        hid_v = jax.nn.gelu(a_v[rows, :] + b_v[nxt, :], approximate=True).astype(BF16)
        kc_ref[0, rows, :] = _dot(hid_k, w2k_ref[...]).astype(BF16)
        vct_ref[0, :, rows] = _dot_nt(w2vt_ref[...], hid_v).astype(BF16)


def _compress(page_table, pool, cw):
    b, n_pages = page_table.shape
    page_rows = pool.shape[2]
    rows = n_pages * page_rows
    n_units = rows // CMP_UNIT
    n_slots = min(2, b)
    hid2 = N_KV_A * CMP_HID
    slab_rows = min(CMP_SLAB, n_units) * CMP_UNIT
    assert slab_rows % page_rows == 0 and rows % slab_rows == 0
    const = lambda a: pl.BlockSpec(a.shape, lambda i, pt: (0,) * a.ndim)
    grid_spec = pltpu.PrefetchScalarGridSpec(
        num_scalar_prefetch=1,
        grid=(b,),
        in_specs=[pl.BlockSpec(memory_space=pl.ANY)] + [const(a) for a in cw],
        out_specs=(pl.BlockSpec((1, n_units, KV_HALF), lambda i, pt: (i, 0, 0)),
                   pl.BlockSpec((1, KV_HALF, n_units), lambda i, pt: (i, 0, 0))),
        scratch_shapes=[pltpu.VMEM((n_slots, KV_COLS, rows), F32),
                        pltpu.VMEM((slab_rows, KV_HALF), F32), pltpu.VMEM((slab_rows, KV_HALF), F32),
                        pltpu.VMEM((n_units + 8, hid2), F32), pltpu.VMEM((n_units + 8, hid2), F32),
                        pltpu.VMEM((n_units + 8, hid2), F32), pltpu.VMEM((n_units + 8, hid2), F32),
                        pltpu.SemaphoreType.DMA((n_slots,))],
    )
    return pl.pallas_call(
        _compress_kernel,
        grid_spec=grid_spec,
        out_shape=(jax.ShapeDtypeStruct((b, n_units, KV_HALF), BF16),
                   jax.ShapeDtypeStruct((b, KV_HALF, n_units), BF16)),
        compiler_params=_params("arbitrary"),
        name="compress",
    )(page_table, pool, *cw)


REMOVED = -3.0e38


def _overlap_matrix(n_blocks, n_units):
    cstart = jnp.arange(n_units) * CMP_STRIDE
    bstart = jnp.arange(n_blocks) * SEL_BLOCK
    ov = (cstart[None, :] < bstart[:, None] + SEL_BLOCK) & (cstart[None, :] + CMP_LEN > bstart[:, None])
    return ov.astype(BF16)


def _dot_01(m01, x):
    hi = x.astype(BF16)
    r1 = x - hi.astype(F32)
    mid = r1.astype(BF16)
    lo = (r1 - mid.astype(F32)).astype(BF16)
    return _dot(m01, hi) + _dot(m01, mid) + _dot(m01, lo)


def _softmax_cols(s, mask):
    s = jnp.where(mask, s, NEG)
    mx = jnp.max(s, axis=0, keepdims=True)
    e = jnp.where(mask, jnp.exp(s - mx), 0.0)
    return e / jnp.maximum(jnp.sum(e, axis=0, keepdims=True), 1e-30)


def _online_init(rows, lanes):
    return (jnp.full((1, lanes), NEG, F32), jnp.zeros((1, lanes), F32), jnp.zeros((rows, lanes), F32))


def _online_update(carry, s, mask, pv):
    m_old, l_old, acc = carry
    s = jnp.where(mask, s, NEG)
    m_new = jnp.maximum(m_old, jnp.max(s, axis=0, keepdims=True))
    alpha = jnp.exp(m_old - m_new)
    p = jnp.where(mask, jnp.exp(s - m_new), 0.0)
    l_new = alpha * l_old + jnp.sum(p, axis=0, keepdims=True)
    return m_new, l_new, alpha * acc + pv(p.astype(BF16))


def _online_finish(carry):
    _, l, acc = carry
    return acc / jnp.maximum(l, 1e-30)


def _select_blocks(imp, qpos):
    blk = lax.broadcasted_iota(jnp.int32, imp.shape, 0)
    n_blocks = imp.shape[0]
    cur = qpos // SEL_BLOCK
    forced = (blk == 0) | (blk == cur) | (blk == cur - 1)
    imp = imp + jnp.where(forced, FORCE_BONUS, 0.0)
    work = jnp.where(blk * SEL_BLOCK <= qpos, imp, NEG)
    sel = jnp.zeros(imp.shape, F32)
    for _ in range(N_SEL):
        mx = jnp.max(work, axis=0, keepdims=True)
        first = jnp.min(jnp.where(work == mx, blk, n_blocks), axis=0, keepdims=True)
        hit = blk == first
        sel = jnp.where(hit, 1.0, sel)
        work = jnp.where(hit, REMOVED, work)
    return sel


PROMPT_KV_CHUNK = 1024
PROMPT_KV_PIECE = 256
CHUNK_BLOCKS = PROMPT_KV_CHUNK // SEL_BLOCK
WIN_BLOCKS = WINDOW // Q_BLOCK + 1
SCORE_FLOOR = -1.0e20


def _block_onehot(t):
    blk = (jnp.arange(t) // SEL_BLOCK) % CHUNK_BLOCKS
    return (blk[:, None] == jnp.arange(LANES)[None, :]).astype(BF16)


def _online_update_biased(carry, s_list, pv_list):
    for s, pv in zip(s_list, pv_list):
        m_old, l_old, acc = carry
        m_new = jnp.maximum(m_old, jnp.max(s, axis=0, keepdims=True))
        alpha = jnp.exp(m_old - m_new)
        p = jnp.exp(s - m_new)
        carry = (m_new, alpha * l_old + jnp.sum(p, axis=0, keepdims=True), alpha * acc + pv(p.astype(BF16)))
    return carry


def _nsa_prompt_kernel(qt_ref, kc_ref, vct_ref, ovl_ref, ks_ref, vst_ref, *rest):
    kw_refs = rest[0:WIN_BLOCKS]
    vwt_refs = rest[WIN_BLOCKS:2 * WIN_BLOCKS]
    gt_ref, o_ref, sel_ref = rest[2 * WIN_BLOCKS:]
    i = pl.program_id(0)
    start = i * Q_BLOCK
    lanes = GROUP_R * Q_BLOCK
    qpos1 = start + lax.broadcasted_iota(jnp.int32, (1, Q_BLOCK), 1)
    qpos = jnp.concatenate([qpos1] * GROUP_R, axis=1)
    n_units = kc_ref.shape[0]
    ck = PROMPT_KV_CHUNK
    piece = PROMPT_KV_PIECE
    for g in range(N_KV_A):
        top = jnp.concatenate([qt_ref[(g * GROUP_R + r) * HEAD_DIM:(g * GROUP_R + r + 1) * HEAD_DIM, :]
                               for r in range(GROUP_R)], axis=1)
        zero = jnp.zeros_like(top)
        qg = jnp.concatenate([top, zero] if g == 0 else [zero, top], axis=0)
        rows_g = slice(g * HEAD_DIM, (g + 1) * HEAD_DIM)
        unit = lax.broadcasted_iota(jnp.int32, (n_units, lanes), 0)
        p_c = _softmax_cols(_dot(kc_ref[...], qg), unit * CMP_STRIDE + (CMP_LEN - 1) <= qpos)
        o_c = _dot(vct_ref[rows_g, :], p_c.astype(BF16))
        p_sum = p_c[:, 0:Q_BLOCK]
        for r in range(1, GROUP_R):
            p_sum = p_sum + p_c[:, r * Q_BLOCK:(r + 1) * Q_BLOCK]
        sel_ref[...] = (1.0 - _select_blocks(_dot_01(ovl_ref[...], p_sum), qpos1)) * NEG

        def sel_scores(c):
            bias = sel_ref[pl.ds(pl.multiple_of(c * CHUNK_BLOCKS, CHUNK_BLOCKS), CHUNK_BLOCKS), :]
            bias = jnp.concatenate([bias] * GROUP_R, axis=1).astype(BF16)
            w = jnp.concatenate([qg, bias, jnp.zeros((KV_HALF - CHUNK_BLOCKS, lanes), BF16)], axis=0)
            k0 = pl.multiple_of(c * ck, ck)
            starts = [k0 + h * piece for h in range(ck // piece)]
            s_list = [_dot(ks_ref[pl.ds(k, piece), :], w) for k in starts]
            pv_list = [lambda p, k=k: _dot(vst_ref[rows_g, pl.ds(k, piece)], p) for k in starts]
            return starts, s_list, pv_list

        def sel_step(c, carry):
            _, s_list, pv_list = sel_scores(c)
            return _online_update_biased(carry, s_list, pv_list)

        n_full = start // ck
        carry = (jnp.full((1, lanes), SCORE_FLOOR, F32),) + _online_init(HEAD_DIM, lanes)[1:]
        carry = lax.fori_loop(0, n_full, sel_step, carry)
        starts, s_list, pv_list = sel_scores(n_full)
        row = lax.broadcasted_iota(jnp.int32, (piece, lanes), 0)
        s_list = [jnp.where(k + row <= qpos, s, NEG) for k, s in zip(starts, s_list)]
        o_s = _online_finish(_online_update_biased(carry, s_list, pv_list))
        carry = _online_init(HEAD_DIM, lanes)
        for j in range(WIN_BLOCKS):
            wpos = start - WINDOW + j * Q_BLOCK + lax.broadcasted_iota(jnp.int32, (Q_BLOCK, lanes), 0)
            dpos = qpos - wpos
            mask = (dpos >= 0) & (dpos < WINDOW) & (wpos >= 0)
            vwt = vwt_refs[j]
            carry = _online_update(carry, _dot(kw_refs[j][...], qg), mask, lambda p, vwt=vwt: _dot(vwt[rows_g, :], p))
        o_w = _online_finish(carry)
        for r in range(GROUP_R):
            h = g * GROUP_R + r
            ln = slice(r * Q_BLOCK, (r + 1) * Q_BLOCK)
            o = (gt_ref[h:h + 1, :] * o_c[:, ln] + gt_ref[N_HEADS_A + h:N_HEADS_A + h + 1, :] * o_s[:, ln]
                 + gt_ref[2 * N_HEADS_A + h:2 * N_HEADS_A + h + 1, :] * o_w[:, ln])
            o_ref[h * HEAD_DIM:(h + 1) * HEAD_DIM, :] = o.astype(o_ref.dtype)


def _nsa_prompt(qt, kc, vct, ks, vst, kw, vwt, small_t):
    t = qt.shape[1]
    n_units = kc.shape[0]
    n_blocks = t // SEL_BLOCK
    ovl = _overlap_matrix(n_blocks, n_units)
    assert t % PROMPT_KV_CHUNK == 0
    ks = jnp.concatenate([ks, _block_onehot(t)], axis=1)
    nq = t // Q_BLOCK
    const = lambda a: pl.BlockSpec(a.shape, lambda i: (0, 0))
    back = WINDOW // Q_BLOCK
    kw_specs = [pl.BlockSpec((Q_BLOCK, KV_HALF), lambda i, j=j: (jnp.maximum(i - back + j, 0), 0))
                for j in range(WIN_BLOCKS)]
    vwt_specs = [pl.BlockSpec((KV_HALF, Q_BLOCK), lambda i, j=j: (0, jnp.maximum(i - back + j, 0)))
                 for j in range(WIN_BLOCKS)]
    return pl.pallas_call(
        _nsa_prompt_kernel,
        grid=(nq,),
        in_specs=[pl.BlockSpec((A_WIDTH, Q_BLOCK), lambda i: (0, i)), const(kc), const(vct), const(ovl),
                  const(ks), const(vst)] + kw_specs + vwt_specs
                 + [pl.BlockSpec((SMALL_COLS, Q_BLOCK), lambda i: (0, i))],
        out_specs=pl.BlockSpec((A_WIDTH, Q_BLOCK), lambda i: (0, i)),
        out_shape=jax.ShapeDtypeStruct((A_WIDTH, t), BF16),
        scratch_shapes=[pltpu.VMEM((n_blocks, Q_BLOCK), F32)],
        compiler_params=_params("arbitrary"),
        name="nsa_prompt",
    )(qt, kc, vct, ovl, ks, vst, *([kw] * WIN_BLOCKS), *([vwt] * WIN_BLOCKS), small_t)


NEW_ROWS_PAD = 16


def _split3(x):
    hi = x.astype(BF16)
    r1 = x - hi.astype(F32)
    mid = r1.astype(BF16)
    return hi, mid, (r1 - mid.astype(F32)).astype(BF16)


def _nsa_sample_kernel(pt_ref, pool_ref, qbd_ref, kc_ref, vct_ref, ovl_ref, rep_ref, oh_ref, new_ref, wint_ref,
                       wnew_ref, gl_ref, o_ref, xt, sel_ref, sem, *, t_new):
    slot = _gather_pipeline(pt_ref, pool_ref, xt, sem)
    past = xt.shape[2]
    lanes = LANES
    qbd = qbd_ref[0]
    qpos = past + lax.broadcasted_iota(jnp.int32, (1, lanes), 1) % t_new
    n_units = kc_ref.shape[1]
    unit = lax.broadcasted_iota(jnp.int32, (n_units, lanes), 0)
    p_c = _softmax_cols(_dot(kc_ref[0], qbd), unit * CMP_STRIDE + (CMP_LEN - 1) <= qpos)
    o_c = _dot(vct_ref[0], p_c.astype(BF16))
    imp = _dot_01(ovl_ref[...], p_c)
    imp = sum(_dot(t, rep_ref[...]) for t in _split3(imp))
    sel_ref[...] = (1.0 - _select_blocks(imp, qpos)) * NEG
    ck = PROMPT_KV_CHUNK
    piece = PROMPT_KV_PIECE

    def sel_step(c, carry):
        bias = sel_ref[pl.ds(pl.multiple_of(c * CHUNK_BLOCKS, CHUNK_BLOCKS), CHUNK_BLOCKS), :].astype(BF16)
        w = jnp.concatenate([qbd, bias, jnp.zeros((KV_HALF - CHUNK_BLOCKS, lanes), BF16)], axis=0)
        k0 = pl.multiple_of(c * ck, ck)
        s_list, pv_list = [], []
        for h in range(ck // piece):
            k = k0 + h * piece
            kt = xt[slot, 0:KV_HALF, pl.ds(k, piece)].astype(BF16)
            s_list.append(_dot_tn(jnp.concatenate([kt, oh_ref[:, h * piece:(h + 1) * piece]], axis=0), w))
            pv_list.append(lambda p, k=k: _dot(xt[slot, KV_HALF:KV_COLS, pl.ds(k, piece)].astype(BF16), p))
        return _online_update_biased(carry, s_list, pv_list)

    carry = (jnp.full((1, lanes), SCORE_FLOOR, F32),) + _online_init(KV_HALF, lanes)[1:]
    carry = lax.fori_loop(0, past // ck, sel_step, carry)

    def new_rows(ref):
        x = ref[0]
        pad = jnp.zeros((NEW_ROWS_PAD - t_new, KV_COLS), F32)
        x = jnp.concatenate([x, pad], axis=0).astype(BF16)
        return x[:, 0:KV_HALF], x[:, KV_HALF:KV_COLS]

    row = lax.broadcasted_iota(jnp.int32, (NEW_ROWS_PAD, lanes), 0)
    new_ok = (row < t_new) & (past + row <= qpos)
    k_new, v_new = new_rows(new_ref)
    s_new = _dot(k_new, qbd) + sel_ref[pl.ds(past // SEL_BLOCK, 1), :]
    carry = _online_update_biased(carry, [jnp.where(new_ok, s_new, NEG)], [lambda p: _dot_tn(v_new, p)])
    o_s = _online_finish(carry)
    n_win = wint_ref.shape[2]
    wpos = past - n_win + lax.broadcasted_iota(jnp.int32, (n_win, lanes), 0)
    dpos = qpos - wpos
    w_mask = (dpos >= 0) & (dpos < WINDOW) & (wpos >= 0)
    s_w = _dot_tn(wint_ref[0, 0:KV_HALF, :].astype(BF16), qbd)
    carry = _online_update(_online_init(KV_HALF, lanes), s_w, w_mask,
                           lambda p: _dot(wint_ref[0, KV_HALF:KV_COLS, :].astype(BF16), p))
    k_wn, v_wn = new_rows(wnew_ref)
    carry = _online_update(carry, _dot(k_wn, qbd), new_ok & (qpos - past - row < WINDOW), lambda p: _dot_tn(v_wn, p))
    o_w = _online_finish(carry)
    o_ref[0] = gl_ref[0, 0:1, :] * o_c + gl_ref[0, 1:2, :] * o_s + gl_ref[0, 2:3, :] * o_w


def _nsa_sample(page_table, pool, q, kc, vct, kvs_new, win_state, kvw_new, gates):
    b, n_pages = page_table.shape
    t_new = q.shape[1]
    page_rows = pool.shape[2]
    past = n_pages * page_rows
    n_units = kc.shape[1]
    assert t_new <= SEL_BLOCK and t_new <= NEW_ROWS_PAD and past % PROMPT_KV_CHUNK == 0
    oh = _block_onehot(PROMPT_KV_CHUNK).T
    used = N_KV_A * GROUP_R * t_new
    assert used <= LANES
    qs = q.reshape(b, t_new, N_KV_A, GROUP_R, HEAD_DIM).transpose(0, 2, 4, 3, 1)
    qs = qs.reshape(b, N_KV_A, HEAD_DIM, GROUP_R * t_new)
    qbd = jnp.zeros((b, N_KV_A, HEAD_DIM, N_KV_A, GROUP_R * t_new), BF16)
    for g in range(N_KV_A):
        qbd = qbd.at[:, g, :, g, :].set(qs[:, g])
    qbd = jnp.pad(qbd.reshape(b, KV_HALF, used), ((0, 0), (0, 0), (0, LANES - used)))
    lane = jnp.arange(LANES)
    same = ((lane[:, None] // (GROUP_R * t_new) == lane[None, :] // (GROUP_R * t_new))
            & (lane[:, None] % t_new == lane[None, :] % t_new) & (lane[:, None] < used) & (lane[None, :] < used))
    rep = same.astype(BF16)
    gl = gates.reshape(b, t_new, 3, N_KV_A, GROUP_R).transpose(0, 2, 3, 4, 1).reshape(b, 3, used)
    gl = jnp.pad(gl, ((0, 0), (0, 8 - 3), (0, LANES - used)))
    n_blocks = -(-(past + t_new) // SEL_BLOCK)
    n_blocks_pad = -(-n_blocks // 8) * 8
    ovl = _overlap_matrix(n_blocks_pad, n_units)
    n_slots = min(2, b)
    const = lambda a: pl.BlockSpec(a.shape, lambda i, pt: (0,) * a.ndim)
    per_seq = lambda a: pl.BlockSpec((1,) + a.shape[1:], lambda i, pt: (i,) + (0,) * (a.ndim - 1))
    kvs_new = kvs_new.reshape(b, t_new, KV_COLS)
    kvw_new = kvw_new.reshape(b, t_new, KV_COLS)
    grid_spec = pltpu.PrefetchScalarGridSpec(
        num_scalar_prefetch=1,
        grid=(b,),
        in_specs=[pl.BlockSpec(memory_space=pl.ANY), per_seq(qbd), per_seq(kc), per_seq(vct), const(ovl), const(rep),
                  const(oh), per_seq(kvs_new), per_seq(win_state), per_seq(kvw_new), per_seq(gl)],
        out_specs=pl.BlockSpec((1, KV_HALF, LANES), lambda i, pt: (i, 0, 0)),
        scratch_shapes=[pltpu.VMEM((n_slots, KV_COLS, past), F32), pltpu.VMEM((n_blocks_pad, LANES), F32),
                        pltpu.SemaphoreType.DMA((n_slots,))],
    )
    o = pl.pallas_call(
        functools.partial(_nsa_sample_kernel, t_new=t_new),
        grid_spec=grid_spec,
        out_shape=jax.ShapeDtypeStruct((b, KV_HALF, LANES), F32),
        compiler_params=_params("arbitrary"),
        name="nsa_sample",
    )(page_table, pool, qbd, kc, vct, ovl, rep, oh, kvs_new, win_state, kvw_new, gl)
    o = o[:, :, :used].reshape(b, N_KV_A, HEAD_DIM, N_KV_A, GROUP_R, t_new)
    o = jnp.stack([o[:, g, :, g] for g in range(N_KV_A)], axis=1)
    return o.transpose(0, 4, 1, 3, 2).reshape(b * t_new, A_WIDTH)


CONV_PAD = 8


def _log_sigmoid(x):
    return jnp.minimum(x, 0.0) - jnp.log(1.0 + jnp.exp(-jnp.abs(x)))


def _mlstm_kernel(u_ref, vm_ref, om_ref, sm_ref, ift_ref, cw_ref, cb_ref, wq_ref, wk_ref, mg_ref,
                  c0_ref, n0_ref, m0_ref, conv0_ref, hm_ref, c_out, n_out, m_out, conv_out,
                  c_s, n_s, m_s, ext_s):
    c = pl.program_id(1)
    chunk = u_ref.shape[0]
    keep = CONV_W - 1

    @pl.when(c == 0)
    def _():
        c_s[...] = c0_ref[0]
        n_s[...] = n0_ref[0]
        m_s[...] = m0_ref[0]
        ext_s[0:CONV_PAD, :] = conv0_ref[0]

    ext_s[CONV_PAD:CONV_PAD + chunk, :] = u_ref[...]
    conv = cb_ref[...]
    for j in range(CONV_W):
        conv = conv + ext_s[pl.ds(CONV_PAD - keep + j, chunk), :] * cw_ref[j:j + 1, :]
    ext_s[CONV_PAD - keep:CONV_PAD, :] = ext_s[CONV_PAD + chunk - keep:CONV_PAD + chunk, :]
    xc = (conv * jax.nn.sigmoid(conv)).astype(BF16)
    ti = lax.broadcasted_iota(jnp.int32, (chunk, chunk), 0)
    si = lax.broadcasted_iota(jnp.int32, (chunk, chunk), 1)
    causal = si <= ti
    head_lane = lax.broadcasted_iota(jnp.int32, (1, LANES), 1)
    sm = sm_ref[...]
    m_all = m_s[...]
    for h in range(N_HEADS_M):
        cols = slice(h * HEAD_DIM_M, (h + 1) * HEAD_DIM_M)
        xh = xc[:, cols]
        qm = _dot(xh, wq_ref[h])
        km = _dot(xh, wk_ref[h]) * HEAD_DIM_M ** -0.5
        qb, kb = qm.astype(BF16), km.astype(BF16)
        vb = vm_ref[:, cols].astype(BF16)
        ig_col = sm[:, SM_IG + h:SM_IG + h + 1]
        lf_col = _log_sigmoid(sm[:, SM_FG + h:SM_FG + h + 1])
        ig_row = ift_ref[0, 0, h:h + 1, :]
        lf_row = _log_sigmoid(ift_ref[0, 0, N_HEADS_M + h:N_HEADS_M + h + 1, :])
        fcum_col = jnp.sum(jnp.where(causal, lf_row, 0.0), axis=1, keepdims=True)
        fcum_row = jnp.sum(jnp.where(ti <= si, lf_col, 0.0), axis=0, keepdims=True)
        d_row = ig_row - fcum_row
        cmax_col = jnp.max(jnp.where(causal, d_row, NEG), axis=1, keepdims=True)
        m_prev = m_all[:, h:h + 1]
        m_row = fcum_col + jnp.maximum(m_prev, cmax_col)
        w_intra = jnp.exp(jnp.where(causal, (fcum_col - m_row) + d_row, NEG))
        w_inter = jnp.exp(m_prev + fcum_col - m_row)
        c_prev = c_s[h]
        n_prev = n_s[h:h + 1, :]
        qk = _dot_nt(qb, kb) * w_intra
        num = w_inter * _dot(qb, c_prev.astype(BF16)) + _dot(qk.astype(BF16), vb)
        den = w_inter * jnp.sum(qm * n_prev, axis=1, keepdims=True) + jnp.sum(qk, axis=1, keepdims=True)
        hh = num / jnp.maximum(jnp.abs(den), jnp.exp(-m_row))
        m_new = m_row[chunk - 1:chunk, :]
        f_last = fcum_col[chunk - 1:chunk, :]
        w_keep = jnp.exp(m_prev + f_last - m_new)
        w_src = jnp.exp(ig_col + f_last - fcum_col - m_new)
        ks = km * w_src
        c_s[h] = w_keep * c_prev + _dot_tn(ks.astype(BF16), vb)
        n_s[h:h + 1, :] = w_keep * n_prev + jnp.sum(ks, axis=0, keepdims=True)
        m_all = jnp.where(head_lane == h, m_new, m_all)
        y = hh * lax.rsqrt(jnp.mean(hh * hh, axis=1, keepdims=True) + EPS) * mg_ref[:, cols]
        hm_ref[:, cols] = y * jax.nn.sigmoid(om_ref[:, cols])
    m_s[...] = m_all

    @pl.when(c == pl.num_programs(1) - 1)
    def _():
        c_out[0] = c_s[...]
        n_out[0] = n_s[...]
        m_out[0] = m_s[...]
        conv_out[0] = ext_s[0:CONV_PAD, :]


def _mlstm(u, vm, om, small, conv_w, conv_b, w_mq, w_mk, mnorm_g, c0, n0, m0, conv0, b, t, chunk):
    n_chunks = t // chunk
    ift = small[:, SM_IG:SM_IG + 2 * N_HEADS_M].reshape(b, n_chunks, chunk, 2 * N_HEADS_M).transpose(0, 1, 3, 2)
    m0p = jnp.pad(m0.reshape(b, 1, N_HEADS_M), ((0, 0), (0, 0), (0, LANES - N_HEADS_M)))
    conv0p = jnp.pad(conv0, ((0, 0), (CONV_PAD - (CONV_W - 1), 0), (0, 0)))
    row = lambda w: pl.BlockSpec((chunk, w), lambda i, c: (i * n_chunks + c, 0))
    const = lambda a: pl.BlockSpec(a.shape, lambda i, c: (0,) * a.ndim)
    per_seq = lambda a: pl.BlockSpec((1,) + a.shape[1:], lambda i, c: (i,) + (0,) * (a.ndim - 1))
    wq = w_mq.astype(BF16)
    wk = w_mk.astype(BF16)
    cb = conv_b.reshape(1, M_WIDTH)
    mg = mnorm_g.reshape(1, M_WIDTH)
    out_shape = (jax.ShapeDtypeStruct((b * t, M_WIDTH), F32),
                 jax.ShapeDtypeStruct(c0.shape, F32), jax.ShapeDtypeStruct(n0.shape, F32),
                 jax.ShapeDtypeStruct(m0p.shape, F32), jax.ShapeDtypeStruct(conv0p.shape, F32))
    hm, c_new, n_new, m_new, conv_new = pl.pallas_call(
        _mlstm_kernel,
        grid=(b, n_chunks),
        in_specs=[row(M_WIDTH), row(M_WIDTH), row(M_WIDTH), row(SMALL_COLS),
                  pl.BlockSpec((1, 1, 2 * N_HEADS_M, chunk), lambda i, c: (i, c, 0, 0)),
                  const(conv_w), const(cb), const(wq), const(wk), const(mg),
                  per_seq(c0), per_seq(n0), per_seq(m0p), per_seq(conv0p)],
        out_specs=(row(M_WIDTH), per_seq(c0), per_seq(n0), per_seq(m0p), per_seq(conv0p)),
        out_shape=out_shape,
        scratch_shapes=[pltpu.VMEM(c0.shape[1:], F32), pltpu.VMEM(n0.shape[1:], F32), pltpu.VMEM((1, LANES), F32),
                        pltpu.VMEM((CONV_PAD + chunk, M_WIDTH), F32)],
        compiler_params=_params("arbitrary", "arbitrary"),
        name="mlstm",
    )(u, vm, om, small, ift, conv_w, cb, wq, wk, mg, c0, n0, m0p, conv0p)
    return hm, c_new, n_new, m_new[:, 0, :N_HEADS_M], conv_new[:, CONV_PAD - (CONV_W - 1):, :]


def _outproj_kernel(x_ref, oa_ref, hm_ref, wo_ref, g1_ref, ng_ref, sc_ref, sh_ref, wq_ref,
                    x1_ref, h2_ref, pq_ref, *, oa_transposed):
    oa = oa_ref[...].astype(BF16)
    wo_a = wo_ref[0:A_WIDTH, :]
    mix = _dot_tn(oa, wo_a) if oa_transposed else _dot(oa, wo_a)
    mix = mix + _dot(hm_ref[...].astype(BF16), wo_ref[A_WIDTH:A_WIDTH + M_WIDTH, :])
    x1 = x_ref[...] + g1_ref[...] * mix
    x1_ref[...] = x1
    y = x1 * lax.rsqrt(jnp.mean(x1 * x1, axis=-1, keepdims=True) + EPS) * ng_ref[...]
    h2 = (y * (1.0 + sc_ref[...]) + sh_ref[...]).astype(BF16)
    h2_ref[...] = h2
    pq_ref[...] = _dot(h2, wq_ref[...]).astype(BF16)


def _outproj(x, oa, hm, w_out, g1, norm_g, sc, sh, peer_wq, tm, oa_transposed):
    r, d = x.shape
    per_row = g1.shape[0] != 1
    mod = pl.BlockSpec((tm, d), lambda i: (i, 0)) if per_row else pl.BlockSpec((1, d), lambda i: (0, 0))
    row = lambda w: pl.BlockSpec((tm, w), lambda i: (i, 0))
    const = lambda a: pl.BlockSpec(a.shape, lambda i: (0, 0))
    oa_spec = pl.BlockSpec((A_WIDTH, tm), lambda i: (0, i)) if oa_transposed else row(A_WIDTH)
    nq = peer_wq.shape[1]
    return pl.pallas_call(
        functools.partial(_outproj_kernel, oa_transposed=oa_transposed),
        grid=(r // tm,),
        in_specs=[row(d), oa_spec, row(M_WIDTH), const(w_out), mod, const(norm_g), mod, mod, const(peer_wq)],
        out_specs=(row(d), row(d), row(nq)),
        out_shape=(jax.ShapeDtypeStruct((r, d), F32), jax.ShapeDtypeStruct((r, d), BF16),
                   jax.ShapeDtypeStruct((r, nq), BF16)),
        compiler_params=_params("arbitrary"),
        name="outproj",
    )(x, oa, hm, w_out, g1, norm_g, sc, sh, peer_wq)


PEER_CHUNK_I = 8
PEER_HALF = PEER_DKEY // 2


def _top_values(work, n):
    rows = lax.broadcasted_iota(jnp.int32, work.shape, 0)
    vals = []
    for _ in range(n):
        mx = jnp.max(work, axis=0, keepdims=True)
        first = jnp.min(jnp.where(work == mx, rows, work.shape[0]), axis=0, keepdims=True)
        work = jnp.where(rows == first, REMOVED, work)
        vals.append(mx)
    return jnp.concatenate(vals, axis=0)


def _peer_kernel(pq_ref, keys_ref, h2t_ref, u_ref, vt_ref, x1_ref, g2_ref, ng_ref, y_ref,
                 s_s, top_s, thr_s, e1_s, e2_s, act_s, w_s, acc_s):
    j = pl.program_id(1)
    tt = pq_ref.shape[0]
    k = PEER_TOPK

    @pl.when(j == 0)
    def _():
        def score(hc, _):
            col = pl.multiple_of(hc * PEER_HALF, PEER_HALF)
            s = _dot_nt(keys_ref[hc], pq_ref[:, pl.ds(col, PEER_HALF)])
            s_s[hc] = s
            top_s[hc] = _top_values(s, k)
            return 0
        lax.fori_loop(0, 2 * PEER_HEADS, score, 0)

        def candidates(x, y):
            pieces = [x[0:1] * y]
            pieces += [x[i:i + 1] * y[0:k // 2] for i in range(1, k // 2)]
            pieces += [x[k // 2:k] * y[0:1]]
            return jnp.concatenate(pieces, axis=0)

        def head(h, _):
            a = top_s[2 * h]
            b = top_s[2 * h + 1]
            ea = jnp.exp(a - a[0:1])
            eb = jnp.exp(b - b[0:1])
            z = jnp.sum(_top_values(candidates(ea, eb), k), axis=0, keepdims=True)
            thr = _top_values(candidates(ea, eb / z), k)[k - 1:k]
            thr_s[h] = jnp.broadcast_to(thr, thr_s.shape[1:])
            e1_s[h] = jnp.exp(s_s[2 * h] - a[0:1])
            e2_s[h] = jnp.exp(s_s[2 * h + 1] - b[0:1]) / z
            return 0
        lax.fori_loop(0, PEER_HEADS, head, 0)
        acc_s[...] = jnp.zeros(acc_s.shape, F32)

    act_s[...] = jax.nn.gelu(_dot(u_ref[...], h2t_ref[...]), approximate=True)

    def build(ii, _):
        i = j * PEER_CHUNK_I + ii
        g = jnp.zeros((PEER_NKEYS, tt), F32)
        for h in range(PEER_HEADS):
            val = e2_s[h] * e1_s[h, pl.ds(i, 1), :]
            g = g + jnp.where(val >= thr_s[h, 0:1, :], val, 0.0)
        rows = pl.ds(pl.multiple_of(ii * PEER_NKEYS, PEER_NKEYS), PEER_NKEYS)
        w_s[rows, :] = (g * act_s[rows, :]).astype(BF16)
        return 0
    lax.fori_loop(0, PEER_CHUNK_I, build, 0)
    acc_s[...] += _dot(vt_ref[...], w_s[...])

    @pl.when(j == pl.num_programs(1) - 1)
    def _():
        x2 = x1_ref[...] + g2_ref[...] * acc_s[...].T
        y_ref[...] = x2 * lax.rsqrt(jnp.mean(x2 * x2, axis=-1, keepdims=True) + EPS) * ng_ref[...]


def _peer(pq, keys, h2t, u_tab, vt_tab, x1, g2, norm_g, tt):
    n_tok, d = x1.shape
    n_exp = u_tab.shape[0]
    ec = PEER_CHUNK_I * PEER_NKEYS
    per_row = g2.shape[0] != 1
    mod = pl.BlockSpec((tt, d), lambda t, j: (t, 0)) if per_row else pl.BlockSpec((1, d), lambda t, j: (0, 0))
    f32_tok = lambda n: pltpu.VMEM((n, PEER_NKEYS, tt), F32)
    return pl.pallas_call(
        _peer_kernel,
        grid=(n_tok // tt, n_exp // ec),
        in_specs=[pl.BlockSpec((tt, pq.shape[1]), lambda t, j: (t, 0)),
                  pl.BlockSpec(keys.shape, lambda t, j: (0, 0, 0)),
                  pl.BlockSpec((d, tt), lambda t, j: (0, t)),
                  pl.BlockSpec((ec, d), lambda t, j: (j, 0)),
                  pl.BlockSpec((d, ec), lambda t, j: (0, j)),
                  pl.BlockSpec((tt, d), lambda t, j: (t, 0)), mod,
                  pl.BlockSpec((1, d), lambda t, j: (0, 0))],
        out_specs=pl.BlockSpec((tt, d), lambda t, j: (t, 0)),
        out_shape=jax.ShapeDtypeStruct((n_tok, d), F32),
        scratch_shapes=[f32_tok(2 * PEER_HEADS), pltpu.VMEM((2 * PEER_HEADS, PEER_TOPK, tt), F32),
                        pltpu.VMEM((PEER_HEADS, 8, tt), F32), f32_tok(PEER_HEADS), f32_tok(PEER_HEADS),
                        pltpu.VMEM((ec, tt), F32), pltpu.VMEM((ec, tt), BF16), pltpu.VMEM((d, tt), F32)],
        compiler_params=_params("arbitrary", "arbitrary"),
        name="peer",
    )(pq, keys, h2t, u_tab, vt_tab, x1, g2, norm_g)


def kernel(x_prompt, x_sample, cache_cmp_kv, cache_slc_kv, page_table, state_win_kv, state_C, state_n, state_m, state_conv, c_prompt, c_sample, w_ada, b_ada, norm1_g, w_in, b_gate_a, cmp_pe_k, cmp_w1_k, cmp_w2_k, cmp_pe_v, cmp_w1_v, cmp_w2_v, conv_w, conv_b, w_mq, w_mk, b_i, b_f, mnorm_g, w_out, norm2_g, peer_wq, peer_keys, peer_u, peer_v, normf_g):
    b_p, t_p, dm = x_prompt.shape
    b_s, t_s, _ = x_sample.shape
    page_rows = cache_cmp_kv.shape[1]
    past_len = page_table.shape[1] * page_rows
    kv_shape = (2, N_KV_A, HEAD_DIM)

    c_all = jnp.concatenate([c_prompt, c_sample], axis=0)
    ada = _ada(jnp.pad(c_all, ((0, -c_all.shape[0] % 8), (0, 0))), w_ada, b_ada)
    w_in_p, bias_small = _permute_w_in(w_in, b_gate_a, b_i, b_f)
    cw = _compress_weights(cmp_pe_k, cmp_w1_k, cmp_w2_k, cmp_pe_v, cmp_w1_v, cmp_w2_v)
    w_out_b = w_out.astype(BF16)
    wq_b = peer_wq.astype(BF16)
    keys_b = peer_keys.reshape(2 * PEER_HEADS, PEER_NKEYS, PEER_HALF).astype(BF16)
    u_b = peer_u.astype(BF16)
    vt_b = peer_v.T.astype(BF16)
    row_vec = lambda v: v.reshape(1, -1)

    def modulation(rows, t):
        parts = [rows[:, k * dm:(k + 1) * dm] for k in range(6)]
        return parts if rows.shape[0] == 1 else [jnp.repeat(p, t, axis=0) for p in parts]

    def tile_rows(n):
        return min(256, n)

    def peer_stage(pq, h2, x1, g2):
        n = x1.shape[0]
        tt = min(512, -(-n // LANES) * LANES)
        pad = -n % tt
        padr = lambda a: jnp.pad(a, ((0, pad), (0, 0)))
        g2p = g2 if g2.shape[0] == 1 else padr(g2)
        y = _peer(padr(pq), keys_b, padr(h2).T, u_b, vt_b, padr(x1), g2p, row_vec(normf_g), tt)
        return y[:n]

    sh1, sc1, g1, sh2, sc2, g2 = modulation(ada[0:b_p], t_p)
    xp = x_prompt.reshape(b_p * t_p, dm)
    pos_p = jnp.tile(jnp.arange(t_p), b_p)
    q, kvc_p, kvs_p, kvw_p, u, vm, om, sm = _inproj(xp, row_vec(norm1_g), sc1, sh1, w_in_p, bias_small,
                                                    _rot_tables(pos_p), tile_rows(b_p * t_p))
    table = jnp.arange(t_p // page_rows, dtype=jnp.int32).reshape(1, -1)
    sm_t = sm.T
    oa_t = []
    for s in range(b_p):
        rows = slice(s * t_p, (s + 1) * t_p)
        pool_p = kvc_p[rows].reshape(t_p // page_rows, page_rows, KV_COLS).transpose(0, 2, 1)
        kc, vct = _compress(table, pool_p, cw)
        ks, kw = kvs_p[rows], kvw_p[rows]
        oa_t.append(_nsa_prompt(q[rows].T, kc[0], vct[0], ks[:, :KV_HALF].astype(BF16),
                                ks[:, KV_HALF:].astype(BF16).T, kw[:, :KV_HALF].astype(BF16),
                                kw[:, KV_HALF:].astype(BF16).T, sm_t[:, rows]))
    oa_t = oa_t[0] if b_p == 1 else jnp.concatenate(oa_t, axis=1)
    hm, c_p, n_p, m_p, conv_p = _mlstm(
        u, vm, om, sm, conv_w, conv_b, w_mq, w_mk, mnorm_g,
        jnp.zeros((b_p, N_HEADS_M, HEAD_DIM_M, HEAD_DIM_M), F32), jnp.zeros((b_p, N_HEADS_M, HEAD_DIM_M), F32),
        jnp.zeros((b_p, N_HEADS_M), F32), jnp.zeros((b_p, CONV_W - 1, M_WIDTH), F32), b_p, t_p, min(128, t_p))
    x1, h2, pq = _outproj(xp, oa_t, hm, w_out_b, g1, row_vec(norm2_g), sc2, sh2, wq_b, tile_rows(b_p * t_p), True)
    y_p = peer_stage(pq, h2, x1, g2).reshape(b_p, t_p, dm)

    sh1, sc1, g1, sh2, sc2, g2 = modulation(ada[b_p:b_p + b_s], t_s)
    xs = x_sample.reshape(b_s * t_s, dm)
    pos_s = jnp.tile(past_len + jnp.arange(t_s), b_s)
    q, kvc_s, kvs_s, kvw_s, u, vm, om, sm = _inproj(xs, row_vec(norm1_g), sc1, sh1, w_in_p, bias_small,
                                                    _rot_tables(pos_s), tile_rows(b_s * t_s))
    kc, vct = _compress(page_table, _transposed_pool(cache_cmp_kv), cw)
    n_win = state_win_kv.shape[1]
    oa = _nsa_sample(page_table, _transposed_pool(cache_slc_kv), q.reshape(b_s, t_s, A_WIDTH),
                     kc, vct, kvs_s, _transposed_pool(state_win_kv), kvw_s,
                     sm[:, 0:GATE_COLS].reshape(b_s, t_s, GATE_COLS))
    hm, c_s, n_s, m_s, conv_s = _mlstm(u, vm, om, sm, conv_w, conv_b, w_mq, w_mk, mnorm_g, state_C.astype(F32),
                                       state_n.astype(F32), state_m.astype(F32), state_conv.astype(F32),
                                       b_s, t_s, t_s)
    x1, h2, pq = _outproj(xs, oa, hm, w_out_b, g1, row_vec(norm2_g), sc2, sh2, wq_b, tile_rows(b_s * t_s), False)
    y_s = peer_stage(pq, h2, x1, g2).reshape(b_s, t_s, dm)

    kv5 = lambda a, b, t: a.reshape((b, t) + kv_shape)
    kvw_p5, kvw_s5 = kv5(kvw_p, b_p, t_p), kv5(kvw_s, b_s, t_s)
    win_p = jnp.concatenate([jnp.zeros((b_p, n_win) + kv_shape, F32), kvw_p5], axis=1)[:, -n_win:]
    win_s = jnp.concatenate([state_win_kv.astype(F32), kvw_s5], axis=1)[:, -n_win:]
    return (y_p, y_s, kv5(kvc_p, b_p, t_p), kv5(kvc_s, b_s, t_s), kv5(kvs_p, b_p, t_p), kv5(kvs_s, b_s, t_s),
            win_p, win_s, c_p, c_s, n_p, n_s, m_p, m_s, conv_p, conv_s)
```

```python
import functools

import jax
import jax.numpy as jnp
from jax import lax
from jax.experimental import pallas as pl
from jax.experimental.pallas import tpu as pltpu

F32 = jnp.float32
BF16 = jnp.bfloat16

N_HEADS_A = 8
N_KV_A = 2
HEAD_DIM = 64
GROUP_R = N_HEADS_A // N_KV_A
ROT_DIM = HEAD_DIM // 4
ROPE_THETA = 500000.0
CMP_LEN = 32
CMP_STRIDE = 16
CMP_HID = 256
SEL_BLOCK = 64
N_SEL = 16
WINDOW = 512
Q_BLOCK = 128
FORCE_BONUS = 1.0e3
N_HEADS_M = 4
HEAD_DIM_M = 128
CONV_W = 4
PEER_HEADS = 8
PEER_NKEYS = 128
PEER_TOPK = 16
PEER_DKEY = 256
EPS = 1e-6
NEG = -1e30

A_WIDTH = N_HEADS_A * HEAD_DIM
M_WIDTH = N_HEADS_M * HEAD_DIM_M
KV_COLS = 2 * N_KV_A * HEAD_DIM
GATE_COLS = 3 * N_HEADS_A
KV_HALF = N_KV_A * HEAD_DIM
LANES = 128
SMALL_COLS = LANES
VMEM_LIMIT = 56 * 1024 * 1024


def _dot(a, b):
    return jnp.dot(a, b, preferred_element_type=F32)


def _dot_nt(a, b):
    return lax.dot_general(a, b, (((1,), (1,)), ((), ())), preferred_element_type=F32)


def _dot_tn(a, b):
    return lax.dot_general(a, b, (((0,), (0,)), ((), ())), preferred_element_type=F32)


def _params(*sem):
    return pltpu.CompilerParams(dimension_semantics=sem, vmem_limit_bytes=VMEM_LIMIT)


def _ada_kernel(c_ref, w_ref, b_ref, o_ref):
    c = c_ref[...]
    s = (c * jax.nn.sigmoid(c)).astype(BF16)
    o_ref[...] = _dot(s, w_ref[...].astype(BF16)) + b_ref[...]


def _ada(c, w_ada, b_ada):
    m, d = c.shape
    n = w_ada.shape[1]
    tn = n // 6
    return pl.pallas_call(
        _ada_kernel,
        grid=(n // tn,),
        in_specs=[pl.BlockSpec((m, d), lambda j: (0, 0)),
                  pl.BlockSpec((d, tn), lambda j: (0, j)),
                  pl.BlockSpec((1, tn), lambda j: (0, j))],
        out_specs=pl.BlockSpec((m, tn), lambda j: (0, j)),
        out_shape=jax.ShapeDtypeStruct((m, n), F32),
        compiler_params=_params("arbitrary"),
        name="ada",
    )(c, w_ada, b_ada.reshape(1, n))


IN_OFF_Q = 0
IN_OFF_KVC = A_WIDTH
IN_OFF_KVS = IN_OFF_KVC + KV_COLS
IN_OFF_KVW = IN_OFF_KVS + KV_COLS
IN_OFF_U = IN_OFF_KVW + KV_COLS
IN_OFF_VM = IN_OFF_U + M_WIDTH
IN_OFF_OM = IN_OFF_VM + M_WIDTH
IN_OFF_SMALL = IN_OFF_OM + M_WIDTH
IN_COLS_PADDED = IN_OFF_SMALL + SMALL_COLS
SM_IG = GATE_COLS
SM_FG = GATE_COLS + N_HEADS_M


def _permute_w_in(w_in, b_gate_a, b_i, b_f):
    d = w_in.shape[0]
    o = 0
    parts = {}
    for name, width in (("q", A_WIDTH), ("kvc", KV_COLS), ("kvs", KV_COLS), ("kvw", KV_COLS), ("g", GATE_COLS),
                        ("u", M_WIDTH), ("vm", M_WIDTH), ("om", M_WIDTH), ("i", N_HEADS_M), ("f", N_HEADS_M)):
        parts[name] = w_in[:, o:o + width]
        o += width
    pad = jnp.zeros((d, SMALL_COLS - GATE_COLS - 2 * N_HEADS_M), w_in.dtype)
    w = jnp.concatenate([parts[k] for k in ("q", "kvc", "kvs", "kvw", "u", "vm", "om", "g", "i", "f")] + [pad], axis=1)
    bias = jnp.concatenate([b_gate_a, b_i, b_f, jnp.zeros((SMALL_COLS - GATE_COLS - 2 * N_HEADS_M,), F32)])
    return w.astype(BF16), bias.reshape(1, SMALL_COLS)


def _rot_tables(pos):
    t = pos.shape[0]
    inv = ROPE_THETA ** (-jnp.arange(0, ROT_DIM, 2, dtype=F32) / ROT_DIM)
    ang = pos.astype(F32)[:, None] * inv[None, :]
    cos, sin = jnp.cos(ang), jnp.sin(ang)
    half = ROT_DIM // 2
    ones = jnp.ones((t, HEAD_DIM - ROT_DIM), F32)
    zeros = jnp.zeros((t, HEAD_DIM - ROT_DIM), F32)
    zh = jnp.zeros((t, half), F32)
    c = jnp.concatenate([cos, cos, ones], axis=1)
    s_lo = jnp.concatenate([-sin, zh, zeros], axis=1)
    s_hi = jnp.concatenate([zh, sin, zeros], axis=1)
    rep = LANES // HEAD_DIM
    return jnp.tile(c, (1, rep)), jnp.tile(s_lo, (1, rep)), jnp.tile(s_hi, (1, rep))


def _inproj_kernel(x_ref, g_ref, sc_ref, sh_ref, w_ref, bsm_ref, rc_ref, rlo_ref, rhi_ref,
                   q_ref, kvc_ref, kvs_ref, kvw_ref, u_ref, vm_ref, om_ref, sm_ref):
    x = x_ref[...]
    y = x * lax.rsqrt(jnp.mean(x * x, axis=-1, keepdims=True) + EPS) * g_ref[...]
    hb = (y * (1.0 + sc_ref[...]) + sh_ref[...]).astype(BF16)
    rc, rlo, rhi = rc_ref[...], rlo_ref[...], rhi_ref[...]
    half = ROT_DIM // 2

    def rot(z):
        return z * rc + pltpu.roll(z, LANES - half, 1) * rlo + pltpu.roll(z, half, 1) * rhi

    zq = _dot(hb, w_ref[:, IN_OFF_Q:IN_OFF_Q + A_WIDTH])
    scale = HEAD_DIM ** -0.5
    for c in range(A_WIDTH // LANES):
        q_ref[:, c * LANES:(c + 1) * LANES] = (rot(zq[:, c * LANES:(c + 1) * LANES]) * scale).astype(BF16)
    for ref, off in ((kvc_ref, IN_OFF_KVC), (kvs_ref, IN_OFF_KVS), (kvw_ref, IN_OFF_KVW)):
        z = _dot(hb, w_ref[:, off:off + KV_COLS])
        ref[:, 0:KV_HALF] = rot(z[:, 0:KV_HALF])
        ref[:, KV_HALF:KV_COLS] = z[:, KV_HALF:KV_COLS]
    u_ref[...] = _dot(hb, w_ref[:, IN_OFF_U:IN_OFF_U + M_WIDTH])
    vm_ref[...] = _dot(hb, w_ref[:, IN_OFF_VM:IN_OFF_VM + M_WIDTH])
    om_ref[...] = _dot(hb, w_ref[:, IN_OFF_OM:IN_OFF_OM + M_WIDTH])
    zs = _dot(hb, w_ref[:, IN_OFF_SMALL:IN_OFF_SMALL + SMALL_COLS]) + bsm_ref[...]
    lane = lax.broadcasted_iota(jnp.int32, zs.shape, 1)
    sm_ref[...] = jnp.where(lane < GATE_COLS, jax.nn.sigmoid(zs), zs)


def _inproj(x, norm_g, sc, sh, w_in_p, bias_small, rot_tabs, tm):
    r, d = x.shape
    per_row = sc.shape[0] != 1
    mod_spec = pl.BlockSpec((tm, d), lambda i: (i, 0)) if per_row else pl.BlockSpec((1, d), lambda i: (0, 0))
    row = lambda w: pl.BlockSpec((tm, w), lambda i: (i, 0))
    const = lambda a: pl.BlockSpec(a.shape, lambda i: (0, 0))
    out_shape = (jax.ShapeDtypeStruct((r, A_WIDTH), BF16),
                 jax.ShapeDtypeStruct((r, KV_COLS), F32), jax.ShapeDtypeStruct((r, KV_COLS), F32),
                 jax.ShapeDtypeStruct((r, KV_COLS), F32),
                 jax.ShapeDtypeStruct((r, M_WIDTH), F32), jax.ShapeDtypeStruct((r, M_WIDTH), F32),
                 jax.ShapeDtypeStruct((r, M_WIDTH), F32), jax.ShapeDtypeStruct((r, SMALL_COLS), F32))
    return pl.pallas_call(
        _inproj_kernel,
        grid=(r // tm,),
        in_specs=[row(d), const(norm_g), mod_spec, mod_spec, const(w_in_p), const(bias_small),
                  row(LANES), row(LANES), row(LANES)],
        out_specs=(row(A_WIDTH), row(KV_COLS), row(KV_COLS), row(KV_COLS), row(M_WIDTH), row(M_WIDTH),
                   row(M_WIDTH), row(SMALL_COLS)),
        out_shape=out_shape,
        compiler_params=_params("arbitrary"),
        name="inproj",
    )(x, norm_g, sc, sh, w_in_p, bias_small, *rot_tabs)


def _transposed_pool(rows5):
    n, r = rows5.shape[0], rows5.shape[1]
    return rows5.transpose(0, 2, 3, 4, 1).reshape(n, KV_COLS, r)


def _page_copy(pt_ref, pool_ref, xt, sem, seq, slot, p):
    page_rows = pool_ref.shape[2]
    dst = pl.ds(pl.multiple_of(p * page_rows, page_rows), page_rows)
    return pltpu.make_async_copy(pool_ref.at[pt_ref[seq, p]], xt.at[slot, :, dst], sem.at[slot])


def _gather_start(pt_ref, pool_ref, xt, sem, seq, slot):
    def body(p, _):
        _page_copy(pt_ref, pool_ref, xt, sem, seq, slot, p).start()
        return 0
    lax.fori_loop(0, pt_ref.shape[1], body, 0)


def _gather_wait(pt_ref, pool_ref, xt, sem, seq, slot):
    def body(p, _):
        _page_copy(pt_ref, pool_ref, xt, sem, seq, slot, p).wait()
        return 0
    lax.fori_loop(0, pt_ref.shape[1], body, 0)


def _gather_pipeline(pt_ref, pool_ref, xt, sem):
    b = pl.program_id(0)
    nb = pl.num_programs(0)
    n_slots = xt.shape[0]
    slot = b % n_slots

    @pl.when(b == 0)
    def _():
        _gather_start(pt_ref, pool_ref, xt, sem, 0, 0)

    if n_slots > 1:
        @pl.when(b + 1 < nb)
        def _():
            _gather_start(pt_ref, pool_ref, xt, sem, b + 1, 1 - slot)

    _gather_wait(pt_ref, pool_ref, xt, sem, b, slot)
    return slot


CMP_UNIT = CMP_STRIDE
CMP_SLAB = 256


def _compress_weights(pe_k, w1_k, w2_k, pe_v, w1_v, w2_v):
    eye = jnp.eye(N_KV_A, dtype=F32)

    def split_w1(w1):
        w = w1.reshape(CMP_LEN, HEAD_DIM, CMP_HID)

        def bd(part):
            return jnp.einsum('ldh,gk->lgdkh', part, eye).reshape(CMP_UNIT * KV_HALF, N_KV_A * CMP_HID).astype(BF16)
        return bd(w[:CMP_UNIT]), bd(w[CMP_UNIT:])

    def bd_w2(w2):
        return jnp.einsum('hd,gk->ghkd', w2, eye).reshape(N_KV_A * CMP_HID, KV_HALF)

    wka, wkb = split_w1(w1_k)
    wva, wvb = split_w1(w1_v)
    pe_a = jnp.concatenate([jnp.tile(pe_k[:CMP_UNIT], (1, N_KV_A)), jnp.tile(pe_v[:CMP_UNIT], (1, N_KV_A))], axis=1)
    pe_b = jnp.concatenate([jnp.tile(pe_k[CMP_UNIT:], (1, N_KV_A)), jnp.tile(pe_v[CMP_UNIT:], (1, N_KV_A))], axis=1)
    return (pe_a, pe_b, wka, wkb, wva, wvb, bd_w2(w2_k).astype(BF16), bd_w2(w2_v).T.astype(BF16))


def _compress_kernel(pt_ref, pool_ref, pea_ref, peb_ref, wka_ref, wkb_ref, wva_ref, wvb_ref, w2k_ref, w2vt_ref,
                     kc_ref, vct_ref, xt, kbuf, vbuf, a_k, b_k, a_v, b_v, sem):
    slot = _gather_pipeline(pt_ref, pool_ref, xt, sem)
    n_units = kc_ref.shape[1]
    slab = min(CMP_SLAB, n_units)
    page_rows = pool_ref.shape[2]
    for s in range(n_units // slab):
        base = s * slab * CMP_UNIT
        for p in range(slab * CMP_UNIT // page_rows):
            x = xt[slot, :, base + p * page_rows:base + (p + 1) * page_rows]
            dst = pl.ds(p * page_rows, page_rows)
            kbuf[dst, :] = x[0:KV_HALF, :].T
            vbuf[dst, :] = x[KV_HALF:KV_COLS, :].T
        zka, zkb, zva, zvb = [], [], [], []
        for l in range(CMP_UNIT):
            xk = kbuf[pl.ds(l, slab, stride=CMP_UNIT), :]
            xv = vbuf[pl.ds(l, slab, stride=CMP_UNIT), :]
            zka.append((xk + pea_ref[l:l + 1, 0:KV_HALF]).astype(BF16))
            zkb.append((xk + peb_ref[l:l + 1, 0:KV_HALF]).astype(BF16))
            zva.append((xv + pea_ref[l:l + 1, KV_HALF:KV_COLS]).astype(BF16))
            zvb.append((xv + peb_ref[l:l + 1, KV_HALF:KV_COLS]).astype(BF16))
        rows = pl.ds(s * slab, slab)
        a_k[rows, :] = _dot(jnp.concatenate(zka, axis=1), wka_ref[...])
        b_k[rows, :] = _dot(jnp.concatenate(zkb, axis=1), wkb_ref[...])
        a_v[rows, :] = _dot(jnp.concatenate(zva, axis=1), wva_ref[...])
        b_v[rows, :] = _dot(jnp.concatenate(zvb, axis=1), wvb_ref[...])
    tail = pl.ds(n_units, 8)
    b_k[tail, :] = jnp.zeros((8, b_k.shape[1]), F32)
    b_v[tail, :] = jnp.zeros((8, b_v.shape[1]), F32)
    for s in range(n_units // slab):
        rows = pl.ds(s * slab, slab)
        nxt = pl.ds(s * slab + 1, slab)
        hid_k = jax.nn.gelu(a_k[rows, :] + b_k[nxt, :], approximate=True).astype(BF16)
        hid_v = jax.nn.gelu(a_v[rows, :] + b_v[nxt, :], approximate=True).astype(BF16)
        kc_ref[0, rows, :] = _dot(hid_k, w2k_ref[...]).astype(BF16)
        vct_ref[0, :, rows] = _dot_nt(w2vt_ref[...], hid_v).astype(BF16)


def _compress(page_table, pool, cw):
    b, n_pages = page_table.shape
    page_rows = pool.shape[2]
    rows = n_pages * page_rows
    n_units = rows // CMP_UNIT
    n_slots = min(2, b)
    hid2 = N_KV_A * CMP_HID
    slab_rows = min(CMP_SLAB, n_units) * CMP_UNIT
    assert slab_rows % page_rows == 0 and rows % slab_rows == 0
    const = lambda a: pl.BlockSpec(a.shape, lambda i, pt: (0,) * a.ndim)
    grid_spec = pltpu.PrefetchScalarGridSpec(
        num_scalar_prefetch=1,
        grid=(b,),
        in_specs=[pl.BlockSpec(memory_space=pl.ANY)] + [const(a) for a in cw],
        out_specs=(pl.BlockSpec((1, n_units, KV_HALF), lambda i, pt: (i, 0, 0)),
                   pl.BlockSpec((1, KV_HALF, n_units), lambda i, pt: (i, 0, 0))),
        scratch_shapes=[pltpu.VMEM((n_slots, KV_COLS, rows), F32),
                        pltpu.VMEM((slab_rows, KV_HALF), F32), pltpu.VMEM((slab_rows, KV_HALF), F32),
                        pltpu.VMEM((n_units + 8, hid2), F32), pltpu.VMEM((n_units + 8, hid2), F32),
                        pltpu.VMEM((n_units + 8, hid2), F32), pltpu.VMEM((n_units + 8, hid2), F32),
                        pltpu.SemaphoreType.DMA((n_slots,))],
    )
    return pl.pallas_call(
        _compress_kernel,
        grid_spec=grid_spec,
        out_shape=(jax.ShapeDtypeStruct((b, n_units, KV_HALF), BF16),
                   jax.ShapeDtypeStruct((b, KV_HALF, n_units), BF16)),
        compiler_params=_params("arbitrary"),
        name="compress",
    )(page_table, pool, *cw)


REMOVED = -3.0e38


def _overlap_matrix(n_blocks, n_units):
    cstart = jnp.arange(n_units) * CMP_STRIDE
    bstart = jnp.arange(n_blocks) * SEL_BLOCK
    ov = (cstart[None, :] < bstart[:, None] + SEL_BLOCK) & (cstart[None, :] + CMP_LEN > bstart[:, None])
    return ov.astype(BF16)


def _dot_01(m01, x):
    hi = x.astype(BF16)
    r1 = x - hi.astype(F32)
    mid = r1.astype(BF16)
    lo = (r1 - mid.astype(F32)).astype(BF16)
    return _dot(m01, hi) + _dot(m01, mid) + _dot(m01, lo)


def _softmax_cols(s, mask):
    s = jnp.where(mask, s, NEG)
    mx = jnp.max(s, axis=0, keepdims=True)
    e = jnp.where(mask, jnp.exp(s - mx), 0.0)
    return e / jnp.maximum(jnp.sum(e, axis=0, keepdims=True), 1e-30)


def _online_init(rows, lanes):
    return (jnp.full((1, lanes), NEG, F32), jnp.zeros((1, lanes), F32), jnp.zeros((rows, lanes), F32))


def _online_update(carry, s, mask, pv):
    m_old, l_old, acc = carry
    s = jnp.where(mask, s, NEG)
    m_new = jnp.maximum(m_old, jnp.max(s, axis=0, keepdims=True))
    alpha = jnp.exp(m_old - m_new)
    p = jnp.where(mask, jnp.exp(s - m_new), 0.0)
    l_new = alpha * l_old + jnp.sum(p, axis=0, keepdims=True)
    return m_new, l_new, alpha * acc + pv(p.astype(BF16))


def _online_finish(carry):
    _, l, acc = carry
    return acc / jnp.maximum(l, 1e-30)


def _select_blocks(imp, qpos):
    blk = lax.broadcasted_iota(jnp.int32, imp.shape, 0)
    n_blocks = imp.shape[0]
    cur = qpos // SEL_BLOCK
    forced = (blk == 0) | (blk == cur) | (blk == cur - 1)
    imp = imp + jnp.where(forced, FORCE_BONUS, 0.0)
    work = jnp.where(blk * SEL_BLOCK <= qpos, imp, NEG)
    sel = jnp.zeros(imp.shape, F32)
    for _ in range(N_SEL):
        mx = jnp.max(work, axis=0, keepdims=True)
        first = jnp.min(jnp.where(work == mx, blk, n_blocks), axis=0, keepdims=True)
        hit = blk == first
        sel = jnp.where(hit, 1.0, sel)
        work = jnp.where(hit, REMOVED, work)
    return sel


PROMPT_KV_CHUNK = 1024
PROMPT_KV_PIECE = 256
CHUNK_BLOCKS = PROMPT_KV_CHUNK // SEL_BLOCK
WIN_BLOCKS = WINDOW // Q_BLOCK + 1
SCORE_FLOOR = -1.0e20


def _block_onehot(t):
    blk = (jnp.arange(t) // SEL_BLOCK) % CHUNK_BLOCKS
    return (blk[:, None] == jnp.arange(LANES)[None, :]).astype(BF16)


def _online_update_biased(carry, s_list, pv_list):
    for s, pv in zip(s_list, pv_list):
        m_old, l_old, acc = carry
        m_new = jnp.maximum(m_old, jnp.max(s, axis=0, keepdims=True))
        alpha = jnp.exp(m_old - m_new)
        p = jnp.exp(s - m_new)
        carry = (m_new, alpha * l_old + jnp.sum(p, axis=0, keepdims=True), alpha * acc + pv(p.astype(BF16)))
    return carry


def _nsa_prompt_kernel(qt_ref, kc_ref, vct_ref, ovl_ref, ks_ref, vst_ref, *rest):
    kw_refs = rest[0:WIN_BLOCKS]
    vwt_refs = rest[WIN_BLOCKS:2 * WIN_BLOCKS]
    gt_ref, o_ref, sel_ref = rest[2 * WIN_BLOCKS:]
    i = pl.program_id(0)
    start = i * Q_BLOCK
    lanes = GROUP_R * Q_BLOCK
    qpos1 = start + lax.broadcasted_iota(jnp.int32, (1, Q_BLOCK), 1)
    qpos = jnp.concatenate([qpos1] * GROUP_R, axis=1)
    n_units = kc_ref.shape[0]
    ck = PROMPT_KV_CHUNK
    piece = PROMPT_KV_PIECE
    for g in range(N_KV_A):
        top = jnp.concatenate([qt_ref[(g * GROUP_R + r) * HEAD_DIM:(g * GROUP_R + r + 1) * HEAD_DIM, :]
                               for r in range(GROUP_R)], axis=1)
        zero = jnp.zeros_like(top)
        qg = jnp.concatenate([top, zero] if g == 0 else [zero, top], axis=0)
        rows_g = slice(g * HEAD_DIM, (g + 1) * HEAD_DIM)
        unit = lax.broadcasted_iota(jnp.int32, (n_units, lanes), 0)
        p_c = _softmax_cols(_dot(kc_ref[...], qg), unit * CMP_STRIDE + (CMP_LEN - 1) <= qpos)
        o_c = _dot(vct_ref[rows_g, :], p_c.astype(BF16))
        p_sum = p_c[:, 0:Q_BLOCK]
        for r in range(1, GROUP_R):
            p_sum = p_sum + p_c[:, r * Q_BLOCK:(r + 1) * Q_BLOCK]
        sel_ref[...] = (1.0 - _select_blocks(_dot_01(ovl_ref[...], p_sum), qpos1)) * NEG

        def sel_scores(c):
            bias = sel_ref[pl.ds(pl.multiple_of(c * CHUNK_BLOCKS, CHUNK_BLOCKS), CHUNK_BLOCKS), :]
            bias = jnp.concatenate([bias] * GROUP_R, axis=1).astype(BF16)
            w = jnp.concatenate([qg, bias, jnp.zeros((KV_HALF - CHUNK_BLOCKS, lanes), BF16)], axis=0)
            k0 = pl.multiple_of(c * ck, ck)
            starts = [k0 + h * piece for h in range(ck // piece)]
            s_list = [_dot(ks_ref[pl.ds(k, piece), :], w) for k in starts]
            pv_list = [lambda p, k=k: _dot(vst_ref[rows_g, pl.ds(k, piece)], p) for k in starts]
            return starts, s_list, pv_list

        def sel_step(c, carry):
            _, s_list, pv_list = sel_scores(c)
            return _online_update_biased(carry, s_list, pv_list)

        n_full = start // ck
        carry = (jnp.full((1, lanes), SCORE_FLOOR, F32),) + _online_init(HEAD_DIM, lanes)[1:]
        carry = lax.fori_loop(0, n_full, sel_step, carry)
        starts, s_list, pv_list = sel_scores(n_full)
        row = lax.broadcasted_iota(jnp.int32, (piece, lanes), 0)
        s_list = [jnp.where(k + row <= qpos, s, NEG) for k, s in zip(starts, s_list)]
        o_s = _online_finish(_online_update_biased(carry, s_list, pv_list))
        carry = _online_init(HEAD_DIM, lanes)
        for j in range(WIN_BLOCKS):
            wpos = start - WINDOW + j * Q_BLOCK + lax.broadcasted_iota(jnp.int32, (Q_BLOCK, lanes), 0)
            dpos = qpos - wpos
            mask = (dpos >= 0) & (dpos < WINDOW) & (wpos >= 0)
            vwt = vwt_refs[j]
            carry = _online_update(carry, _dot(kw_refs[j][...], qg), mask, lambda p, vwt=vwt: _dot(vwt[rows_g, :], p))
        o_w = _online_finish(carry)
        for r in range(GROUP_R):
            h = g * GROUP_R + r
            ln = slice(r * Q_BLOCK, (r + 1) * Q_BLOCK)
            o = (gt_ref[h:h + 1, :] * o_c[:, ln] + gt_ref[N_HEADS_A + h:N_HEADS_A + h + 1, :] * o_s[:, ln]
                 + gt_ref[2 * N_HEADS_A + h:2 * N_HEADS_A + h + 1, :] * o_w[:, ln])
            o_ref[h * HEAD_DIM:(h + 1) * HEAD_DIM, :] = o.astype(o_ref.dtype)


def _nsa_prompt(qt, kc, vct, ks, vst, kw, vwt, small_t):
    t = qt.shape[1]
    n_units = kc.shape[0]
    n_blocks = t // SEL_BLOCK
    ovl = _overlap_matrix(n_blocks, n_units)
    assert t % PROMPT_KV_CHUNK == 0
    ks = jnp.concatenate([ks, _block_onehot(t)], axis=1)
    nq = t // Q_BLOCK
    const = lambda a: pl.BlockSpec(a.shape, lambda i: (0, 0))
    back = WINDOW // Q_BLOCK
    kw_specs = [pl.BlockSpec((Q_BLOCK, KV_HALF), lambda i, j=j: (jnp.maximum(i - back + j, 0), 0))
                for j in range(WIN_BLOCKS)]
    vwt_specs = [pl.BlockSpec((KV_HALF, Q_BLOCK), lambda i, j=j: (0, jnp.maximum(i - back + j, 0)))
                 for j in range(WIN_BLOCKS)]
    return pl.pallas_call(
        _nsa_prompt_kernel,
        grid=(nq,),
        in_specs=[pl.BlockSpec((A_WIDTH, Q_BLOCK), lambda i: (0, i)), const(kc), const(vct), const(ovl),
                  const(ks), const(vst)] + kw_specs + vwt_specs
                 + [pl.BlockSpec((SMALL_COLS, Q_BLOCK), lambda i: (0, i))],
        out_specs=pl.BlockSpec((A_WIDTH, Q_BLOCK), lambda i: (0, i)),
        out_shape=jax.ShapeDtypeStruct((A_WIDTH, t), BF16),
        scratch_shapes=[pltpu.VMEM((n_blocks, Q_BLOCK), F32)],
        compiler_params=_params("arbitrary"),
        name="nsa_prompt",
    )(qt, kc, vct, ovl, ks, vst, *([kw] * WIN_BLOCKS), *([vwt] * WIN_BLOCKS), small_t)


NEW_ROWS_PAD = 16


def _split3(x):
    hi = x.astype(BF16)
    r1 = x - hi.astype(F32)
    mid = r1.astype(BF16)
    return hi, mid, (r1 - mid.astype(F32)).astype(BF16)


def _nsa_sample_kernel(pt_ref, pool_ref, qbd_ref, kc_ref, vct_ref, ovl_ref, rep_ref, oh_ref, new_ref, wint_ref,
                       wnew_ref, gl_ref, o_ref, xt, sel_ref, sem, *, t_new):
    slot = _gather_pipeline(pt_ref, pool_ref, xt, sem)
    past = xt.shape[2]
    lanes = LANES
    qbd = qbd_ref[0]
    qpos = past + lax.broadcasted_iota(jnp.int32, (1, lanes), 1) % t_new
    n_units = kc_ref.shape[1]
    unit = lax.broadcasted_iota(jnp.int32, (n_units, lanes), 0)
    p_c = _softmax_cols(_dot(kc_ref[0], qbd), unit * CMP_STRIDE + (CMP_LEN - 1) <= qpos)
    o_c = _dot(vct_ref[0], p_c.astype(BF16))
    imp = _dot_01(ovl_ref[...], p_c)
    imp = sum(_dot(t, rep_ref[...]) for t in _split3(imp))
    sel_ref[...] = (1.0 - _select_blocks(imp, qpos)) * NEG
    ck = PROMPT_KV_CHUNK
    piece = PROMPT_KV_PIECE

    def sel_step(c, carry):
        bias = sel_ref[c * CHUNK_BLOCKS:(c + 1) * CHUNK_BLOCKS, :].astype(BF16)
        w = jnp.concatenate([qbd, bias, jnp.zeros((KV_HALF - CHUNK_BLOCKS, lanes), BF16)], axis=0)
        k0 = c * ck
        s_list, pv_list = [], []
        for h in range(ck // piece):
            k = k0 + h * piece
            kt = xt[slot, 0:KV_HALF, pl.ds(k, piece)].astype(BF16)
            s_list.append(_dot_tn(jnp.concatenate([kt, oh_ref[:, h * piece:(h + 1) * piece]], axis=0), w))
            pv_list.append(lambda p, k=k: _dot(xt[slot, KV_HALF:KV_COLS, pl.ds(k, piece)].astype(BF16), p))
        return _online_update_biased(carry, s_list, pv_list)

    carry = (jnp.full((1, lanes), SCORE_FLOOR, F32),) + _online_init(KV_HALF, lanes)[1:]
    for c in range(past // ck):
        carry = sel_step(c, carry)

    def new_rows(ref):
        x = ref[0]
        pad = jnp.zeros((NEW_ROWS_PAD - t_new, KV_COLS), F32)
        x = jnp.concatenate([x, pad], axis=0).astype(BF16)
        return x[:, 0:KV_HALF], x[:, KV_HALF:KV_COLS]

    row = lax.broadcasted_iota(jnp.int32, (NEW_ROWS_PAD, lanes), 0)
    new_ok = (row < t_new) & (past + row <= qpos)
    k_new, v_new = new_rows(new_ref)
    s_new = _dot(k_new, qbd) + sel_ref[pl.ds(past // SEL_BLOCK, 1), :]
    carry = _online_update_biased(carry, [jnp.where(new_ok, s_new, NEG)], [lambda p: _dot_tn(v_new, p)])
    o_s = _online_finish(carry)
    n_win = wint_ref.shape[2]
    wpos = past - n_win + lax.broadcasted_iota(jnp.int32, (n_win, lanes), 0)
    dpos = qpos - wpos
    w_mask = (dpos >= 0) & (dpos < WINDOW) & (wpos >= 0)
    s_w = _dot_tn(wint_ref[0, 0:KV_HALF, :].astype(BF16), qbd)
    carry = _online_update(_online_init(KV_HALF, lanes), s_w, w_mask,
                           lambda p: _dot(wint_ref[0, KV_HALF:KV_COLS, :].astype(BF16), p))
    k_wn, v_wn = new_rows(wnew_ref)
    carry = _online_update(carry, _dot(k_wn, qbd), new_ok & (qpos - past - row < WINDOW), lambda p: _dot_tn(v_wn, p))
    o_w = _online_finish(carry)
    o_ref[0] = gl_ref[0, 0:1, :] * o_c + gl_ref[0, 1:2, :] * o_s + gl_ref[0, 2:3, :] * o_w


def _nsa_sample(page_table, pool, q, kc, vct, kvs_new, win_state, kvw_new, gates):
    b, n_pages = page_table.shape
    t_new = q.shape[1]
    page_rows = pool.shape[2]
    past = n_pages * page_rows
    n_units = kc.shape[1]
    assert t_new <= SEL_BLOCK and t_new <= NEW_ROWS_PAD and past % PROMPT_KV_CHUNK == 0
    oh = _block_onehot(PROMPT_KV_CHUNK).T
    used = N_KV_A * GROUP_R * t_new
    assert used <= LANES
    qs = q.reshape(b, t_new, N_KV_A, GROUP_R, HEAD_DIM).transpose(0, 2, 4, 3, 1)
    qs = qs.reshape(b, N_KV_A, HEAD_DIM, GROUP_R * t_new)
    qbd = jnp.zeros((b, N_KV_A, HEAD_DIM, N_KV_A, GROUP_R * t_new), BF16)
    for g in range(N_KV_A):
        qbd = qbd.at[:, g, :, g, :].set(qs[:, g])
    qbd = jnp.pad(qbd.reshape(b, KV_HALF, used), ((0, 0), (0, 0), (0, LANES - used)))
    lane = jnp.arange(LANES)
    same = ((lane[:, None] // (GROUP_R * t_new) == lane[None, :] // (GROUP_R * t_new))
            & (lane[:, None] % t_new == lane[None, :] % t_new) & (lane[:, None] < used) & (lane[None, :] < used))
    rep = same.astype(BF16)
    gl = gates.reshape(b, t_new, 3, N_KV_A, GROUP_R).transpose(0, 2, 3, 4, 1).reshape(b, 3, used)
    gl = jnp.pad(gl, ((0, 0), (0, 8 - 3), (0, LANES - used)))
    n_blocks = -(-(past + t_new) // SEL_BLOCK)
    n_blocks_pad = -(-n_blocks // 8) * 8
    ovl = _overlap_matrix(n_blocks_pad, n_units)
    n_slots = min(2, b)
    const = lambda a: pl.BlockSpec(a.shape, lambda i, pt: (0,) * a.ndim)
    per_seq = lambda a: pl.BlockSpec((1,) + a.shape[1:], lambda i, pt: (i,) + (0,) * (a.ndim - 1))
    kvs_new = kvs_new.reshape(b, t_new, KV_COLS)
    kvw_new = kvw_new.reshape(b, t_new, KV_COLS)
    grid_spec = pltpu.PrefetchScalarGridSpec(
        num_scalar_prefetch=1,
        grid=(b,),
        in_specs=[pl.BlockSpec(memory_space=pl.ANY), per_seq(qbd), per_seq(kc), per_seq(vct), const(ovl), const(rep),
                  const(oh), per_seq(kvs_new), per_seq(win_state), per_seq(kvw_new), per_seq(gl)],
        out_specs=pl.BlockSpec((1, KV_HALF, LANES), lambda i, pt: (i, 0, 0)),
        scratch_shapes=[pltpu.VMEM((n_slots, KV_COLS, past), F32), pltpu.VMEM((n_blocks_pad, LANES), F32),
                        pltpu.SemaphoreType.DMA((n_slots,))],
    )
    o = pl.pallas_call(
        functools.partial(_nsa_sample_kernel, t_new=t_new),
        grid_spec=grid_spec,
        out_shape=jax.ShapeDtypeStruct((b, KV_HALF, LANES), F32),
        compiler_params=_params("arbitrary"),
        name="nsa_sample",
    )(page_table, pool, qbd, kc, vct, ovl, rep, oh, kvs_new, win_state, kvw_new, gl)
    o = o[:, :, :used].reshape(b, N_KV_A, HEAD_DIM, N_KV_A, GROUP_R, t_new)
    o = jnp.stack([o[:, g, :, g] for g in range(N_KV_A)], axis=1)
    return o.transpose(0, 4, 1, 3, 2).reshape(b * t_new, A_WIDTH)


CONV_PAD = 8


def _log_sigmoid(x):
    return jnp.minimum(x, 0.0) - jnp.log(1.0 + jnp.exp(-jnp.abs(x)))


def _mlstm_kernel(u_ref, vm_ref, om_ref, sm_ref, ift_ref, cw_ref, cb_ref, wq_ref, wk_ref, mg_ref,
                  c0_ref, n0_ref, m0_ref, conv0_ref, hm_ref, c_out, n_out, m_out, conv_out,
                  c_s, n_s, m_s, ext_s):
    c = pl.program_id(1)
    chunk = u_ref.shape[0]
    keep = CONV_W - 1

    @pl.when(c == 0)
    def _():
        c_s[...] = c0_ref[0]
        n_s[...] = n0_ref[0]
        m_s[...] = m0_ref[0]
        ext_s[0:CONV_PAD, :] = conv0_ref[0]

    ext_s[CONV_PAD:CONV_PAD + chunk, :] = u_ref[...]
    conv = cb_ref[...]
    for j in range(CONV_W):
        conv = conv + ext_s[pl.ds(CONV_PAD - keep + j, chunk), :] * cw_ref[j:j + 1, :]
    ext_s[CONV_PAD - keep:CONV_PAD, :] = ext_s[CONV_PAD + chunk - keep:CONV_PAD + chunk, :]
    xc = (conv * jax.nn.sigmoid(conv)).astype(BF16)
    ti = lax.broadcasted_iota(jnp.int32, (chunk, chunk), 0)
    si = lax.broadcasted_iota(jnp.int32, (chunk, chunk), 1)
    causal = si <= ti
    head_lane = lax.broadcasted_iota(jnp.int32, (1, LANES), 1)
    sm = sm_ref[...]
    m_all = m_s[...]
    for h in range(N_HEADS_M):
        cols = slice(h * HEAD_DIM_M, (h + 1) * HEAD_DIM_M)
        xh = xc[:, cols]
        qm = _dot(xh, wq_ref[h])
        km = _dot(xh, wk_ref[h]) * HEAD_DIM_M ** -0.5
        qb, kb = qm.astype(BF16), km.astype(BF16)
        vb = vm_ref[:, cols].astype(BF16)
        ig_col = sm[:, SM_IG + h:SM_IG + h + 1]
        lf_col = _log_sigmoid(sm[:, SM_FG + h:SM_FG + h + 1])
        ig_row = ift_ref[0, 0, h:h + 1, :]
        lf_row = _log_sigmoid(ift_ref[0, 0, N_HEADS_M + h:N_HEADS_M + h + 1, :])
        fcum_col = jnp.sum(jnp.where(causal, lf_row, 0.0), axis=1, keepdims=True)
        fcum_row = jnp.sum(jnp.where(ti <= si, lf_col, 0.0), axis=0, keepdims=True)
        d_row = ig_row - fcum_row
        cmax_col = jnp.max(jnp.where(causal, d_row, NEG), axis=1, keepdims=True)
        m_prev = m_all[:, h:h + 1]
        m_row = fcum_col + jnp.maximum(m_prev, cmax_col)
        w_intra = jnp.exp(jnp.where(causal, (fcum_col - m_row) + d_row, NEG))
        w_inter = jnp.exp(m_prev + fcum_col - m_row)
        c_prev = c_s[h]
        n_prev = n_s[h:h + 1, :]
        qk = _dot_nt(qb, kb) * w_intra
        num = w_inter * _dot(qb, c_prev.astype(BF16)) + _dot(qk.astype(BF16), vb)
        den = w_inter * jnp.sum(qm * n_prev, axis=1, keepdims=True) + jnp.sum(qk, axis=1, keepdims=True)
        hh = num / jnp.maximum(jnp.abs(den), jnp.exp(-m_row))
        m_new = m_row[chunk - 1:chunk, :]
        f_last = fcum_col[chunk - 1:chunk, :]
        w_keep = jnp.exp(m_prev + f_last - m_new)
        w_src = jnp.exp(ig_col + f_last - fcum_col - m_new)
        ks = km * w_src
        c_s[h] = w_keep * c_prev + _dot_tn(ks.astype(BF16), vb)
        n_s[h:h + 1, :] = w_keep * n_prev + jnp.sum(ks, axis=0, keepdims=True)
        m_all = jnp.where(head_lane == h, m_new, m_all)
        y = hh * lax.rsqrt(jnp.mean(hh * hh, axis=1, keepdims=True) + EPS) * mg_ref[:, cols]
        hm_ref[:, cols] = y * jax.nn.sigmoid(om_ref[:, cols])
    m_s[...] = m_all

    @pl.when(c == pl.num_programs(1) - 1)
    def _():
        c_out[0] = c_s[...]
        n_out[0] = n_s[...]
        m_out[0] = m_s[...]
        conv_out[0] = ext_s[0:CONV_PAD, :]


def _mlstm(u, vm, om, small, conv_w, conv_b, w_mq, w_mk, mnorm_g, c0, n0, m0, conv0, b, t, chunk):
    n_chunks = t // chunk
    ift = small[:, SM_IG:SM_IG + 2 * N_HEADS_M].reshape(b, n_chunks, chunk, 2 * N_HEADS_M).transpose(0, 1, 3, 2)
    m0p = jnp.pad(m0.reshape(b, 1, N_HEADS_M), ((0, 0), (0, 0), (0, LANES - N_HEADS_M)))
    conv0p = jnp.pad(conv0, ((0, 0), (CONV_PAD - (CONV_W - 1), 0), (0, 0)))
    row = lambda w: pl.BlockSpec((chunk, w), lambda i, c: (i * n_chunks + c, 0))
    const = lambda a: pl.BlockSpec(a.shape, lambda i, c: (0,) * a.ndim)
    per_seq = lambda a: pl.BlockSpec((1,) + a.shape[1:], lambda i, c: (i,) + (0,) * (a.ndim - 1))
    wq = w_mq.astype(BF16)
    wk = w_mk.astype(BF16)
    cb = conv_b.reshape(1, M_WIDTH)
    mg = mnorm_g.reshape(1, M_WIDTH)
    out_shape = (jax.ShapeDtypeStruct((b * t, M_WIDTH), F32),
                 jax.ShapeDtypeStruct(c0.shape, F32), jax.ShapeDtypeStruct(n0.shape, F32),
                 jax.ShapeDtypeStruct(m0p.shape, F32), jax.ShapeDtypeStruct(conv0p.shape, F32))
    hm, c_new, n_new, m_new, conv_new = pl.pallas_call(
        _mlstm_kernel,
        grid=(b, n_chunks),
        in_specs=[row(M_WIDTH), row(M_WIDTH), row(M_WIDTH), row(SMALL_COLS),
                  pl.BlockSpec((1, 1, 2 * N_HEADS_M, chunk), lambda i, c: (i, c, 0, 0)),
                  const(conv_w), const(cb), const(wq), const(wk), const(mg),
                  per_seq(c0), per_seq(n0), per_seq(m0p), per_seq(conv0p)],
        out_specs=(row(M_WIDTH), per_seq(c0), per_seq(n0), per_seq(m0p), per_seq(conv0p)),
        out_shape=out_shape,
        scratch_shapes=[pltpu.VMEM(c0.shape[1:], F32), pltpu.VMEM(n0.shape[1:], F32), pltpu.VMEM((1, LANES), F32),
                        pltpu.VMEM((CONV_PAD + chunk, M_WIDTH), F32)],
        compiler_params=_params("arbitrary", "arbitrary"),
        name="mlstm",
    )(u, vm, om, small, ift, conv_w, cb, wq, wk, mg, c0, n0, m0p, conv0p)
    return hm, c_new, n_new, m_new[:, 0, :N_HEADS_M], conv_new[:, CONV_PAD - (CONV_W - 1):, :]


def _outproj_kernel(x_ref, oa_ref, hm_ref, wo_ref, g1_ref, ng_ref, sc_ref, sh_ref, wq_ref,
                    x1_ref, h2_ref, pq_ref, *, oa_transposed):
    oa = oa_ref[...].astype(BF16)
    wo_a = wo_ref[0:A_WIDTH, :]
    mix = _dot_tn(oa, wo_a) if oa_transposed else _dot(oa, wo_a)
    mix = mix + _dot(hm_ref[...].astype(BF16), wo_ref[A_WIDTH:A_WIDTH + M_WIDTH, :])
    x1 = x_ref[...] + g1_ref[...] * mix
    x1_ref[...] = x1
    y = x1 * lax.rsqrt(jnp.mean(x1 * x1, axis=-1, keepdims=True) + EPS) * ng_ref[...]
    h2 = (y * (1.0 + sc_ref[...]) + sh_ref[...]).astype(BF16)
    h2_ref[...] = h2
    pq_ref[...] = _dot(h2, wq_ref[...]).astype(BF16)


def _outproj(x, oa, hm, w_out, g1, norm_g, sc, sh, peer_wq, tm, oa_transposed):
    r, d = x.shape
    per_row = g1.shape[0] != 1
    mod = pl.BlockSpec((tm, d), lambda i: (i, 0)) if per_row else pl.BlockSpec((1, d), lambda i: (0, 0))
    row = lambda w: pl.BlockSpec((tm, w), lambda i: (i, 0))
    const = lambda a: pl.BlockSpec(a.shape, lambda i: (0, 0))
    oa_spec = pl.BlockSpec((A_WIDTH, tm), lambda i: (0, i)) if oa_transposed else row(A_WIDTH)
    nq = peer_wq.shape[1]
    return pl.pallas_call(
        functools.partial(_outproj_kernel, oa_transposed=oa_transposed),
        grid=(r // tm,),
        in_specs=[row(d), oa_spec, row(M_WIDTH), const(w_out), mod, const(norm_g), mod, mod, const(peer_wq)],
        out_specs=(row(d), row(d), row(nq)),
        out_shape=(jax.ShapeDtypeStruct((r, d), F32), jax.ShapeDtypeStruct((r, d), BF16),
                   jax.ShapeDtypeStruct((r, nq), BF16)),
        compiler_params=_params("arbitrary"),
        name="outproj",
    )(x, oa, hm, w_out, g1, norm_g, sc, sh, peer_wq)


PEER_CHUNK_I = 8
PEER_HALF = PEER_DKEY // 2


SUBLANES = 8


def _oddeven_merge(lo, hi, r):
    step = r * 2
    if step < hi - lo:
        yield from _oddeven_merge(lo, hi, step)
        yield from _oddeven_merge(lo + r, hi, step)
        yield from [(i, i + r) for i in range(lo + r, hi - r, step)]
    else:
        yield (lo, lo + r)


def _oddeven_sort(lo, hi):
    if hi - lo >= 1:
        mid = lo + (hi - lo) // 2
        yield from _oddeven_sort(lo, mid)
        yield from _oddeven_sort(mid + 1, hi)
        yield from _oddeven_merge(lo, hi, 1)


def _top_values(x):
    k = PEER_TOPK
    r = [x[v * SUBLANES:(v + 1) * SUBLANES] for v in range(k)]

    def exchange(t, i, j):
        t[i], t[j] = jnp.maximum(t[i], t[j]), jnp.minimum(t[i], t[j])

    for i, j in _oddeven_sort(0, k - 1):
        exchange(r, i, j)
    shift = SUBLANES // 2
    while shift >= 1:
        r = [jnp.maximum(r[v], pltpu.roll(r[k - 1 - v], shift, 0)) for v in range(k)]
        d = k // 2
        while d >= 1:
            for v in range(k):
                if v & d == 0:
                    exchange(r, v, v + d)
            d //= 2
        shift //= 2
    return jnp.concatenate([t[0:1] for t in r], axis=0)


def _peer_kernel(pq_ref, keys_ref, h2t_ref, u_ref, vt_ref, x1_ref, g2_ref, ng_ref, y_ref,
                 s_s, top_s, thr_s, e1_s, e2_s, act_s, w_s, acc_s):
    j = pl.program_id(1)
    tt = pq_ref.shape[0]
    k = PEER_TOPK

    @pl.when(j == 0)
    def _():
        def score(hc, _):
            col = pl.multiple_of(hc * PEER_HALF, PEER_HALF)
            s = _dot_nt(keys_ref[hc], pq_ref[:, pl.ds(col, PEER_HALF)])
            s_s[hc] = s
            top_s[hc] = _top_values(s)
            return 0
        lax.fori_loop(0, 2 * PEER_HEADS, score, 0)

        def candidates(x, y):
            pieces = [x[0:1] * y]
            pieces += [x[i:i + 1] * y[0:k // 2] for i in range(1, k // 2)]
            pieces += [x[k // 2:k] * y[0:1]]
            n_rows = sum(p.shape[0] for p in pieces)
            pieces += [jnp.full((k * SUBLANES - n_rows, tt), -1.0, F32)]
            return jnp.concatenate(pieces, axis=0)

        def head(h, _):
            a = top_s[2 * h]
            b = top_s[2 * h + 1]
            ea = jnp.exp(a - a[0:1])
            eb = jnp.exp(b - b[0:1])
            z = jnp.sum(_top_values(candidates(ea, eb)), axis=0, keepdims=True)
            thr = _top_values(candidates(ea, eb / z))[k - 1:k]
            thr_s[h] = jnp.broadcast_to(thr, thr_s.shape[1:])
            e1_s[h] = jnp.exp(s_s[2 * h] - a[0:1])
            e2_s[h] = jnp.exp(s_s[2 * h + 1] - b[0:1]) / z
            return 0
        lax.fori_loop(0, PEER_HEADS, head, 0)
        acc_s[...] = jnp.zeros(acc_s.shape, F32)

    act_s[...] = jax.nn.gelu(_dot(u_ref[...], h2t_ref[...]), approximate=True)

    def build(ii, _):
        i = j * PEER_CHUNK_I + ii
        r = pl.ds(pl.multiple_of(ii * PEER_NKEYS, PEER_NKEYS), PEER_NKEYS)
        e1_rows = [e1_s[h, pl.ds(i, 1), :] for h in range(PEER_HEADS)]
        for lt in range(tt // LANES):
            cols = slice(lt * LANES, (lt + 1) * LANES)
            g = jnp.zeros((PEER_NKEYS, LANES), F32)
            for h in range(PEER_HEADS):
                val = e2_s[h, :, cols] * e1_rows[h][:, cols]
                g = g + jnp.where(val >= thr_s[h, 0:1, cols], val, 0.0)
            w_s[r, cols] = (g * act_s[r, cols]).astype(BF16)
        return 0
    lax.fori_loop(0, PEER_CHUNK_I, build, 0)
    acc_s[...] += _dot(vt_ref[...], w_s[...])

    @pl.when(j == pl.num_programs(1) - 1)
    def _():
        x2 = x1_ref[...] + g2_ref[...] * acc_s[...].T
        y_ref[...] = x2 * lax.rsqrt(jnp.mean(x2 * x2, axis=-1, keepdims=True) + EPS) * ng_ref[...]


def _peer(pq, keys, h2t, u_tab, vt_tab, x1, g2, norm_g, tt):
    n_tok, d = x1.shape
    n_exp = u_tab.shape[0]
    ec = PEER_CHUNK_I * PEER_NKEYS
    per_row = g2.shape[0] != 1
    mod = pl.BlockSpec((tt, d), lambda t, j: (t, 0)) if per_row else pl.BlockSpec((1, d), lambda t, j: (0, 0))
    f32_tok = lambda n: pltpu.VMEM((n, PEER_NKEYS, tt), F32)
    return pl.pallas_call(
        _peer_kernel,
        grid=(n_tok // tt, n_exp // ec),
        in_specs=[pl.BlockSpec((tt, pq.shape[1]), lambda t, j: (t, 0)),
                  pl.BlockSpec(keys.shape, lambda t, j: (0, 0, 0)),
                  pl.BlockSpec((d, tt), lambda t, j: (0, t)),
                  pl.BlockSpec((ec, d), lambda t, j: (j, 0)),
                  pl.BlockSpec((d, ec), lambda t, j: (0, j)),
                  pl.BlockSpec((tt, d), lambda t, j: (t, 0)), mod,
                  pl.BlockSpec((1, d), lambda t, j: (0, 0))],
        out_specs=pl.BlockSpec((tt, d), lambda t, j: (t, 0)),
        out_shape=jax.ShapeDtypeStruct((n_tok, d), F32),
        scratch_shapes=[f32_tok(2 * PEER_HEADS), pltpu.VMEM((2 * PEER_HEADS, PEER_TOPK, tt), F32),
                        pltpu.VMEM((PEER_HEADS, 8, tt), F32), f32_tok(PEER_HEADS), f32_tok(PEER_HEADS),
                        pltpu.VMEM((ec, tt), F32), pltpu.VMEM((ec, tt), BF16), pltpu.VMEM((d, tt), F32)],
        compiler_params=_params("arbitrary", "arbitrary"),
        name="peer",
    )(pq, keys, h2t, u_tab, vt_tab, x1, g2, norm_g)


def kernel(x_prompt, x_sample, cache_cmp_kv, cache_slc_kv, page_table, state_win_kv, state_C, state_n, state_m, state_conv, c_prompt, c_sample, w_ada, b_ada, norm1_g, w_in, b_gate_a, cmp_pe_k, cmp_w1_k, cmp_w2_k, cmp_pe_v, cmp_w1_v, cmp_w2_v, conv_w, conv_b, w_mq, w_mk, b_i, b_f, mnorm_g, w_out, norm2_g, peer_wq, peer_keys, peer_u, peer_v, normf_g):
    b_p, t_p, dm = x_prompt.shape
    b_s, t_s, _ = x_sample.shape
    page_rows = cache_cmp_kv.shape[1]
    past_len = page_table.shape[1] * page_rows
    kv_shape = (2, N_KV_A, HEAD_DIM)

    c_all = jnp.concatenate([c_prompt, c_sample], axis=0)
    ada = _ada(jnp.pad(c_all, ((0, -c_all.shape[0] % 8), (0, 0))), w_ada, b_ada)
    w_in_p, bias_small = _permute_w_in(w_in, b_gate_a, b_i, b_f)
    cw = _compress_weights(cmp_pe_k, cmp_w1_k, cmp_w2_k, cmp_pe_v, cmp_w1_v, cmp_w2_v)
    w_out_b = w_out.astype(BF16)
    wq_b = peer_wq.astype(BF16)
    keys_b = peer_keys.reshape(2 * PEER_HEADS, PEER_NKEYS, PEER_HALF).astype(BF16)
    u_b = peer_u.astype(BF16)
    vt_b = peer_v.T.astype(BF16)
    row_vec = lambda v: v.reshape(1, -1)

    def modulation(rows, t):
        parts = [rows[:, k * dm:(k + 1) * dm] for k in range(6)]
        return parts if rows.shape[0] == 1 else [jnp.repeat(p, t, axis=0) for p in parts]

    def tile_rows(n):
        return min(256, n)

    def peer_stage(pq, h2, x1, g2):
        n = x1.shape[0]
        tt = min(512, -(-n // LANES) * LANES)
        pad = -n % tt
        padr = lambda a: jnp.pad(a, ((0, pad), (0, 0)))
        g2p = g2 if g2.shape[0] == 1 else padr(g2)
        y = _peer(padr(pq), keys_b, padr(h2).T, u_b, vt_b, padr(x1), g2p, row_vec(normf_g), tt)
        return y[:n]

    sh1, sc1, g1, sh2, sc2, g2 = modulation(ada[0:b_p], t_p)
    xp = x_prompt.reshape(b_p * t_p, dm)
    pos_p = jnp.tile(jnp.arange(t_p), b_p)
    q, kvc_p, kvs_p, kvw_p, u, vm, om, sm = _inproj(xp, row_vec(norm1_g), sc1, sh1, w_in_p, bias_small,
                                                    _rot_tables(pos_p), tile_rows(b_p * t_p))
    table = jnp.arange(t_p // page_rows, dtype=jnp.int32).reshape(1, -1)
    sm_t = sm.T
    oa_t = []
    for s in range(b_p):
        rows = slice(s * t_p, (s + 1) * t_p)
        pool_p = kvc_p[rows].reshape(t_p // page_rows, page_rows, KV_COLS).transpose(0, 2, 1)
        kc, vct = _compress(table, pool_p, cw)
        ks, kw = kvs_p[rows], kvw_p[rows]
        oa_t.append(_nsa_prompt(q[rows].T, kc[0], vct[0], ks[:, :KV_HALF].astype(BF16),
                                ks[:, KV_HALF:].astype(BF16).T, kw[:, :KV_HALF].astype(BF16),
                                kw[:, KV_HALF:].astype(BF16).T, sm_t[:, rows]))
    oa_t = oa_t[0] if b_p == 1 else jnp.concatenate(oa_t, axis=1)
    hm, c_p, n_p, m_p, conv_p = _mlstm(
        u, vm, om, sm, conv_w, conv_b, w_mq, w_mk, mnorm_g,
        jnp.zeros((b_p, N_HEADS_M, HEAD_DIM_M, HEAD_DIM_M), F32), jnp.zeros((b_p, N_HEADS_M, HEAD_DIM_M), F32),
        jnp.zeros((b_p, N_HEADS_M), F32), jnp.zeros((b_p, CONV_W - 1, M_WIDTH), F32), b_p, t_p, min(128, t_p))
    x1, h2, pq = _outproj(xp, oa_t, hm, w_out_b, g1, row_vec(norm2_g), sc2, sh2, wq_b, tile_rows(b_p * t_p), True)
    y_p = peer_stage(pq, h2, x1, g2).reshape(b_p, t_p, dm)

    sh1, sc1, g1, sh2, sc2, g2 = modulation(ada[b_p:b_p + b_s], t_s)
    xs = x_sample.reshape(b_s * t_s, dm)
    pos_s = jnp.tile(past_len + jnp.arange(t_s), b_s)
    q, kvc_s, kvs_s, kvw_s, u, vm, om, sm = _inproj(xs, row_vec(norm1_g), sc1, sh1, w_in_p, bias_small,
                                                    _rot_tables(pos_s), tile_rows(b_s * t_s))
    kc, vct = _compress(page_table, _transposed_pool(cache_cmp_kv), cw)
    n_win = state_win_kv.shape[1]
    oa = _nsa_sample(page_table, _transposed_pool(cache_slc_kv), q.reshape(b_s, t_s, A_WIDTH),
                     kc, vct, kvs_s, _transposed_pool(state_win_kv), kvw_s,
                     sm[:, 0:GATE_COLS].reshape(b_s, t_s, GATE_COLS))
    hm, c_s, n_s, m_s, conv_s = _mlstm(u, vm, om, sm, conv_w, conv_b, w_mq, w_mk, mnorm_g, state_C.astype(F32),
                                       state_n.astype(F32), state_m.astype(F32), state_conv.astype(F32),
                                       b_s, t_s, t_s)
    x1, h2, pq = _outproj(xs, oa, hm, w_out_b, g1, row_vec(norm2_g), sc2, sh2, wq_b, tile_rows(b_s * t_s), False)
    y_s = peer_stage(pq, h2, x1, g2).reshape(b_s, t_s, dm)

    kv5 = lambda a, b, t: a.reshape((b, t) + kv_shape)
    kvw_p5, kvw_s5 = kv5(kvw_p, b_p, t_p), kv5(kvw_s, b_s, t_s)
    win_p = jnp.concatenate([jnp.zeros((b_p, n_win) + kv_shape, F32), kvw_p5], axis=1)[:, -n_win:]
    win_s = jnp.concatenate([state_win_kv.astype(F32), kvw_s5], axis=1)[:, -n_win:]
    return (y_p, y_s, kv5(kvc_p, b_p, t_p), kv5(kvc_s, b_s, t_s), kv5(kvs_p, b_p, t_p), kv5(kvs_s, b_s, t_s),
            win_p, win_s, c_p, c_s, n_p, n_s, m_p, m_s, conv_p, conv_s)
```

```python
import functools

import jax
import jax.numpy as jnp
from jax import lax
from jax.experimental import pallas as pl
from jax.experimental.pallas import tpu as pltpu

F32 = jnp.float32
BF16 = jnp.bfloat16

N_HEADS_A = 8
N_KV_A = 2
HEAD_DIM = 64
GROUP_R = N_HEADS_A // N_KV_A
ROT_DIM = HEAD_DIM // 4
ROPE_THETA = 500000.0
CMP_LEN = 32
CMP_STRIDE = 16
CMP_HID = 256
SEL_BLOCK = 64
N_SEL = 16
WINDOW = 512
Q_BLOCK = 128
FORCE_BONUS = 1.0e3
N_HEADS_M = 4
HEAD_DIM_M = 128
CONV_W = 4
PEER_HEADS = 8
PEER_NKEYS = 128
PEER_TOPK = 16
PEER_DKEY = 256
EPS = 1e-6
NEG = -1e30
LOG2_E = 1.4426950408889634

A_WIDTH = N_HEADS_A * HEAD_DIM
M_WIDTH = N_HEADS_M * HEAD_DIM_M
KV_COLS = 2 * N_KV_A * HEAD_DIM
GATE_COLS = 3 * N_HEADS_A
KV_HALF = N_KV_A * HEAD_DIM
LANES = 128
SMALL_COLS = LANES
VMEM_LIMIT = 56 * 1024 * 1024


def _dot(a, b):
    return jnp.dot(a, b, preferred_element_type=F32)


def _dot_nt(a, b):
    return lax.dot_general(a, b, (((1,), (1,)), ((), ())), preferred_element_type=F32)


def _dot_tn(a, b):
    return lax.dot_general(a, b, (((0,), (0,)), ((), ())), preferred_element_type=F32)


def _params(*sem):
    return pltpu.CompilerParams(dimension_semantics=sem, vmem_limit_bytes=VMEM_LIMIT)


def _ada_kernel(c_ref, w_ref, b_ref, o_ref):
    c = c_ref[...]
    s = (c * jax.nn.sigmoid(c)).astype(BF16)
    o_ref[...] = _dot(s, w_ref[...].astype(BF16)) + b_ref[...]


def _ada(c, w_ada, b_ada):
    m, d = c.shape
    n = w_ada.shape[1]
    tn = n // 6
    return pl.pallas_call(
        _ada_kernel,
        grid=(n // tn,),
        in_specs=[pl.BlockSpec((m, d), lambda j: (0, 0)),
                  pl.BlockSpec((d, tn), lambda j: (0, j)),
                  pl.BlockSpec((1, tn), lambda j: (0, j))],
        out_specs=pl.BlockSpec((m, tn), lambda j: (0, j)),
        out_shape=jax.ShapeDtypeStruct((m, n), F32),
        compiler_params=_params("arbitrary"),
        name="ada",
    )(c, w_ada, b_ada.reshape(1, n))


IN_OFF_Q = 0
IN_OFF_KVC = A_WIDTH
IN_OFF_KVS = IN_OFF_KVC + KV_COLS
IN_OFF_KVW = IN_OFF_KVS + KV_COLS
IN_OFF_U = IN_OFF_KVW + KV_COLS
IN_OFF_VM = IN_OFF_U + M_WIDTH
IN_OFF_OM = IN_OFF_VM + M_WIDTH
IN_OFF_SMALL = IN_OFF_OM + M_WIDTH
IN_COLS_PADDED = IN_OFF_SMALL + SMALL_COLS
SM_IG = GATE_COLS
SM_FG = GATE_COLS + N_HEADS_M


def _permute_w_in(w_in, b_gate_a, b_i, b_f):
    d = w_in.shape[0]
    o = 0
    parts = {}
    for name, width in (("q", A_WIDTH), ("kvc", KV_COLS), ("kvs", KV_COLS), ("kvw", KV_COLS), ("g", GATE_COLS),
                        ("u", M_WIDTH), ("vm", M_WIDTH), ("om", M_WIDTH), ("i", N_HEADS_M), ("f", N_HEADS_M)):
        parts[name] = w_in[:, o:o + width]
        o += width
    pad = jnp.zeros((d, SMALL_COLS - GATE_COLS - 2 * N_HEADS_M), w_in.dtype)
    w = jnp.concatenate([parts[k] for k in ("q", "kvc", "kvs", "kvw", "u", "vm", "om", "g", "i", "f")] + [pad], axis=1)
    bias = jnp.concatenate([b_gate_a, b_i, b_f, jnp.zeros((SMALL_COLS - GATE_COLS - 2 * N_HEADS_M,), F32)])
    return w.astype(BF16), bias.reshape(1, SMALL_COLS)


def _rot_tables(pos):
    t = pos.shape[0]
    inv = ROPE_THETA ** (-jnp.arange(0, ROT_DIM, 2, dtype=F32) / ROT_DIM)
    ang = pos.astype(F32)[:, None] * inv[None, :]
    cos, sin = jnp.cos(ang), jnp.sin(ang)
    half = ROT_DIM // 2
    ones = jnp.ones((t, HEAD_DIM - ROT_DIM), F32)
    zeros = jnp.zeros((t, HEAD_DIM - ROT_DIM), F32)
    zh = jnp.zeros((t, half), F32)
    c = jnp.concatenate([cos, cos, ones], axis=1)
    s_lo = jnp.concatenate([-sin, zh, zeros], axis=1)
    s_hi = jnp.concatenate([zh, sin, zeros], axis=1)
    rep = LANES // HEAD_DIM
    return jnp.tile(c, (1, rep)), jnp.tile(s_lo, (1, rep)), jnp.tile(s_hi, (1, rep))


def _inproj_kernel(x_ref, g_ref, sc_ref, sh_ref, w_ref, bsm_ref, rc_ref, rlo_ref, rhi_ref,
                   q_ref, kvc_ref, kvs_ref, kvw_ref, u_ref, vm_ref, om_ref, sm_ref):
    x = x_ref[...]
    y = x * lax.rsqrt(jnp.mean(x * x, axis=-1, keepdims=True) + EPS) * g_ref[...]
    hb = (y * (1.0 + sc_ref[...]) + sh_ref[...]).astype(BF16)
    rc, rlo, rhi = rc_ref[...], rlo_ref[...], rhi_ref[...]
    half = ROT_DIM // 2

    def rot(z):
        return z * rc + pltpu.roll(z, LANES - half, 1) * rlo + pltpu.roll(z, half, 1) * rhi

    zq = _dot(hb, w_ref[:, IN_OFF_Q:IN_OFF_Q + A_WIDTH])
    scale = HEAD_DIM ** -0.5 * LOG2_E
    for c in range(A_WIDTH // LANES):
        q_ref[:, c * LANES:(c + 1) * LANES] = (rot(zq[:, c * LANES:(c + 1) * LANES]) * scale).astype(BF16)
    for ref, off in ((kvc_ref, IN_OFF_KVC), (kvs_ref, IN_OFF_KVS), (kvw_ref, IN_OFF_KVW)):
        z = _dot(hb, w_ref[:, off:off + KV_COLS])
        ref[:, 0:KV_HALF] = rot(z[:, 0:KV_HALF])
        ref[:, KV_HALF:KV_COLS] = z[:, KV_HALF:KV_COLS]
    u_ref[...] = _dot(hb, w_ref[:, IN_OFF_U:IN_OFF_U + M_WIDTH])
    vm_ref[...] = _dot(hb, w_ref[:, IN_OFF_VM:IN_OFF_VM + M_WIDTH])
    om_ref[...] = _dot(hb, w_ref[:, IN_OFF_OM:IN_OFF_OM + M_WIDTH])
    zs = _dot(hb, w_ref[:, IN_OFF_SMALL:IN_OFF_SMALL + SMALL_COLS]) + bsm_ref[...]
    lane = lax.broadcasted_iota(jnp.int32, zs.shape, 1)
    sm_ref[...] = jnp.where(lane < GATE_COLS, jax.nn.sigmoid(zs), zs)


def _inproj(x, norm_g, sc, sh, w_in_p, bias_small, rot_tabs, tm):
    r, d = x.shape
    per_row = sc.shape[0] != 1
    mod_spec = pl.BlockSpec((tm, d), lambda i: (i, 0)) if per_row else pl.BlockSpec((1, d), lambda i: (0, 0))
    row = lambda w: pl.BlockSpec((tm, w), lambda i: (i, 0))
    const = lambda a: pl.BlockSpec(a.shape, lambda i: (0, 0))
    out_shape = (jax.ShapeDtypeStruct((r, A_WIDTH), BF16),
                 jax.ShapeDtypeStruct((r, KV_COLS), F32), jax.ShapeDtypeStruct((r, KV_COLS), F32),
                 jax.ShapeDtypeStruct((r, KV_COLS), F32),
                 jax.ShapeDtypeStruct((r, M_WIDTH), F32), jax.ShapeDtypeStruct((r, M_WIDTH), F32),
                 jax.ShapeDtypeStruct((r, M_WIDTH), F32), jax.ShapeDtypeStruct((r, SMALL_COLS), F32))
    return pl.pallas_call(
        _inproj_kernel,
        grid=(r // tm,),
        in_specs=[row(d), const(norm_g), mod_spec, mod_spec, const(w_in_p), const(bias_small),
                  row(LANES), row(LANES), row(LANES)],
        out_specs=(row(A_WIDTH), row(KV_COLS), row(KV_COLS), row(KV_COLS), row(M_WIDTH), row(M_WIDTH),
                   row(M_WIDTH), row(SMALL_COLS)),
        out_shape=out_shape,
        compiler_params=_params("arbitrary"),
        name="inproj",
    )(x, norm_g, sc, sh, w_in_p, bias_small, *rot_tabs)


def _transposed_pool(rows5):
    n, r = rows5.shape[0], rows5.shape[1]
    return rows5.transpose(0, 2, 3, 4, 1).reshape(n, KV_COLS, r)


GATHER_UNROLL = 8


def _page_copy(pt_ref, pool_ref, xt, sem, seq, slot, p):
    page_rows = pool_ref.shape[2]
    dst = pl.ds(pl.multiple_of(p * page_rows, page_rows), page_rows)
    return pltpu.make_async_copy(pool_ref.at[pt_ref[seq, p]], xt.at[slot, :, dst], sem.at[slot])


def _gather_start(pt_ref, pool_ref, xt, sem, seq, slot):
    def body(p, _):
        _page_copy(pt_ref, pool_ref, xt, sem, seq, slot, p).start()
        return 0
    lax.fori_loop(0, pt_ref.shape[1], body, 0, unroll=GATHER_UNROLL)


def _gather_wait(pt_ref, pool_ref, xt, sem, seq, slot):
    def body(p, _):
        _page_copy(pt_ref, pool_ref, xt, sem, seq, slot, p).wait()
        return 0
    lax.fori_loop(0, pt_ref.shape[1], body, 0, unroll=GATHER_UNROLL)


def _gather_pipeline(pt_ref, pool_ref, xt, sem):
    b = pl.program_id(0)
    nb = pl.num_programs(0)
    n_slots = xt.shape[0]
    slot = b % n_slots

    @pl.when(b == 0)
    def _():
        _gather_start(pt_ref, pool_ref, xt, sem, 0, 0)

    if n_slots > 1:
        @pl.when(b + 1 < nb)
        def _():
            _gather_start(pt_ref, pool_ref, xt, sem, b + 1, 1 - slot)

    _gather_wait(pt_ref, pool_ref, xt, sem, b, slot)
    return slot


CMP_UNIT = CMP_STRIDE
CMP_SLAB = 256


def _compress_weights(pe_k, w1_k, w2_k, pe_v, w1_v, w2_v):
    eye = jnp.eye(N_KV_A, dtype=F32)

    def split_w1(w1):
        w = w1.reshape(CMP_LEN, HEAD_DIM, CMP_HID)

        def bd(part):
            return jnp.einsum('ldh,gk->lgdkh', part, eye).reshape(CMP_UNIT * KV_HALF, N_KV_A * CMP_HID).astype(BF16)
        return bd(w[:CMP_UNIT]), bd(w[CMP_UNIT:])

    def bd_w2(w2):
        return jnp.einsum('hd,gk->ghkd', w2, eye).reshape(N_KV_A * CMP_HID, KV_HALF)

    wka, wkb = split_w1(w1_k)
    wva, wvb = split_w1(w1_v)
    pe_a = jnp.concatenate([jnp.tile(pe_k[:CMP_UNIT], (1, N_KV_A)), jnp.tile(pe_v[:CMP_UNIT], (1, N_KV_A))], axis=1)
    pe_b = jnp.concatenate([jnp.tile(pe_k[CMP_UNIT:], (1, N_KV_A)), jnp.tile(pe_v[CMP_UNIT:], (1, N_KV_A))], axis=1)
    return (pe_a, pe_b, wka, wkb, wva, wvb, bd_w2(w2_k).astype(BF16), bd_w2(w2_v).T.astype(BF16))


def _compress_kernel(pt_ref, pool_ref, pea_ref, peb_ref, wka_ref, wkb_ref, wva_ref, wvb_ref, w2k_ref, w2vt_ref,
                     kc_ref, vct_ref, xt, kbuf, vbuf, a_k, b_k, a_v, b_v, sem):
    slot = _gather_pipeline(pt_ref, pool_ref, xt, sem)
    n_units = kc_ref.shape[1]
    slab = min(CMP_SLAB, n_units)
    page_rows = pool_ref.shape[2]
    for s in range(n_units // slab):
        base = s * slab * CMP_UNIT
        for p in range(slab * CMP_UNIT // page_rows):
            x = xt[slot, :, base + p * page_rows:base + (p + 1) * page_rows]
            dst = pl.ds(p * page_rows, page_rows)
            kbuf[dst, :] = x[0:KV_HALF, :].T
            vbuf[dst, :] = x[KV_HALF:KV_COLS, :].T
        zka, zkb, zva, zvb = [], [], [], []
        for l in range(CMP_UNIT):
            xk = kbuf[pl.ds(l, slab, stride=CMP_UNIT), :]
            xv = vbuf[pl.ds(l, slab, stride=CMP_UNIT), :]
            zka.append((xk + pea_ref[l:l + 1, 0:KV_HALF]).astype(BF16))
            zkb.append((xk + peb_ref[l:l + 1, 0:KV_HALF]).astype(BF16))
            zva.append((xv + pea_ref[l:l + 1, KV_HALF:KV_COLS]).astype(BF16))
            zvb.append((xv + peb_ref[l:l + 1, KV_HALF:KV_COLS]).astype(BF16))
        rows = pl.ds(s * slab, slab)
        a_k[rows, :] = _dot(jnp.concatenate(zka, axis=1), wka_ref[...])
        b_k[rows, :] = _dot(jnp.concatenate(zkb, axis=1), wkb_ref[...])
        a_v[rows, :] = _dot(jnp.concatenate(zva, axis=1), wva_ref[...])
        b_v[rows, :] = _dot(jnp.concatenate(zvb, axis=1), wvb_ref[...])
    tail = pl.ds(n_units, 8)
    b_k[tail, :] = jnp.zeros((8, b_k.shape[1]), F32)
    b_v[tail, :] = jnp.zeros((8, b_v.shape[1]), F32)
    for s in range(n_units // slab):
        rows = pl.ds(s * slab, slab)
        nxt = pl.ds(s * slab + 1, slab)
        hid_k = jax.nn.gelu(a_k[rows, :] + b_k[nxt, :], approximate=True).astype(BF16)
        hid_v = jax.nn.gelu(a_v[rows, :] + b_v[nxt, :], approximate=True).astype(BF16)
        kc_ref[0, rows, :] = _dot(hid_k, w2k_ref[...]).astype(BF16)
        vct_ref[0, :, rows] = _dot_nt(w2vt_ref[...], hid_v).astype(BF16)


def _compress(page_table, pool, cw):
    b, n_pages = page_table.shape
    page_rows = pool.shape[2]
    rows = n_pages * page_rows
    n_units = rows // CMP_UNIT
    n_slots = min(2, b)
    hid2 = N_KV_A * CMP_HID
    slab_rows = min(CMP_SLAB, n_units) * CMP_UNIT
    assert slab_rows % page_rows == 0 and rows % slab_rows == 0
    const = lambda a: pl.BlockSpec(a.shape, lambda i, pt: (0,) * a.ndim)
    grid_spec = pltpu.PrefetchScalarGridSpec(
        num_scalar_prefetch=1,
        grid=(b,),
        in_specs=[pl.BlockSpec(memory_space=pl.ANY)] + [const(a) for a in cw],
        out_specs=(pl.BlockSpec((1, n_units, KV_HALF), lambda i, pt: (i, 0, 0)),
                   pl.BlockSpec((1, KV_HALF, n_units), lambda i, pt: (i, 0, 0))),
        scratch_shapes=[pltpu.VMEM((n_slots, KV_COLS, rows), F32),
                        pltpu.VMEM((slab_rows, KV_HALF), F32), pltpu.VMEM((slab_rows, KV_HALF), F32),
                        pltpu.VMEM((n_units + 8, hid2), F32), pltpu.VMEM((n_units + 8, hid2), F32),
                        pltpu.VMEM((n_units + 8, hid2), F32), pltpu.VMEM((n_units + 8, hid2), F32),
                        pltpu.SemaphoreType.DMA((n_slots,))],
    )
    return pl.pallas_call(
        _compress_kernel,
        grid_spec=grid_spec,
        out_shape=(jax.ShapeDtypeStruct((b, n_units, KV_HALF), BF16),
                   jax.ShapeDtypeStruct((b, KV_HALF, n_units), BF16)),
        compiler_params=_params("arbitrary"),
        name="compress",
    )(page_table, pool, *cw)


REMOVED = -3.0e38


def _overlap_matrix(n_blocks, n_units):
    cstart = jnp.arange(n_units) * CMP_STRIDE
    bstart = jnp.arange(n_blocks) * SEL_BLOCK
    ov = (cstart[None, :] < bstart[:, None] + SEL_BLOCK) & (cstart[None, :] + CMP_LEN > bstart[:, None])
    return ov.astype(BF16)


def _dot_01(m01, x):
    hi = x.astype(BF16)
    r1 = x - hi.astype(F32)
    mid = r1.astype(BF16)
    lo = (r1 - mid.astype(F32)).astype(BF16)
    return _dot(m01, hi) + _dot(m01, mid) + _dot(m01, lo)


def _softmax_cols(s, mask):
    s = jnp.where(mask, s, NEG)
    mx = jnp.max(s, axis=0, keepdims=True)
    e = jnp.where(mask, jnp.exp2(s - mx), 0.0)
    return e / jnp.maximum(jnp.sum(e, axis=0, keepdims=True), 1e-30)


def _online_init(rows, lanes):
    return (jnp.full((1, lanes), NEG, F32), jnp.zeros((1, lanes), F32), jnp.zeros((rows, lanes), F32))


def _online_update(carry, s, mask, pv):
    m_old, l_old, acc = carry
    s = jnp.where(mask, s, NEG)
    m_new = jnp.maximum(m_old, jnp.max(s, axis=0, keepdims=True))
    alpha = jnp.exp2(m_old - m_new)
    p = jnp.where(mask, jnp.exp2(s - m_new), 0.0)
    l_new = alpha * l_old + jnp.sum(p, axis=0, keepdims=True)
    return m_new, l_new, alpha * acc + pv(p.astype(BF16))


def _online_finish(carry):
    _, l, acc = carry
    return acc / jnp.maximum(l, 1e-30)


def _select_blocks(imp, qpos):
    blk = lax.broadcasted_iota(jnp.int32, imp.shape, 0)
    n_blocks = imp.shape[0]
    cur = qpos // SEL_BLOCK
    forced = (blk == 0) | (blk == cur) | (blk == cur - 1)
    imp = imp + jnp.where(forced, FORCE_BONUS, 0.0)
    work = jnp.where(blk * SEL_BLOCK <= qpos, imp, NEG)
    sel = jnp.zeros(imp.shape, F32)
    for _ in range(N_SEL):
        mx = jnp.max(work, axis=0, keepdims=True)
        first = jnp.min(jnp.where(work == mx, blk, n_blocks), axis=0, keepdims=True)
        hit = blk == first
        sel = jnp.where(hit, 1.0, sel)
        work = jnp.where(hit, REMOVED, work)
    return sel


PROMPT_KV_CHUNK = 1024
PROMPT_KV_PIECE = 128
CHUNK_BLOCKS = PROMPT_KV_CHUNK // SEL_BLOCK
WIN_BLOCKS = WINDOW // Q_BLOCK + 1
SCORE_FLOOR = -1.0e20


def _block_onehot(t):
    blk = (jnp.arange(t) // SEL_BLOCK) % CHUNK_BLOCKS
    return (blk[:, None] == jnp.arange(LANES)[None, :]).astype(BF16)


def _online_update_biased(carry, s_list, pv_list):
    for s, pv in zip(s_list, pv_list):
        m_old, l_old, acc = carry
        m_t, l_t, a_t, p_t = [], [], [], []
        for lt in range(s.shape[1] // LANES):
            cols = slice(lt * LANES, (lt + 1) * LANES)
            st = s[:, cols]
            m_new = jnp.maximum(m_old[:, cols], jnp.max(st, axis=0, keepdims=True))
            alpha = jnp.exp2(m_old[:, cols] - m_new)
            p = jnp.exp2(st - m_new)
            m_t.append(m_new)
            a_t.append(alpha)
            l_t.append(alpha * l_old[:, cols] + jnp.sum(p, axis=0, keepdims=True))
            p_t.append(p.astype(BF16))
        cat = lambda parts: parts[0] if len(parts) == 1 else jnp.concatenate(parts, axis=1)
        carry = (cat(m_t), cat(l_t), cat(a_t) * acc + pv(cat(p_t)))
    return carry


def _nsa_prompt_kernel(qt_ref, kc_ref, vct_ref, ovl_ref, ks_ref, vst_ref, *rest):
    kw_refs = rest[0:WIN_BLOCKS]
    vwt_refs = rest[WIN_BLOCKS:2 * WIN_BLOCKS]
    gt_ref, o_ref, sel_ref = rest[2 * WIN_BLOCKS:]
    i = pl.program_id(0)
    start = i * Q_BLOCK
    lanes = GROUP_R * Q_BLOCK
    qpos1 = start + lax.broadcasted_iota(jnp.int32, (1, Q_BLOCK), 1)
    qpos = jnp.concatenate([qpos1] * GROUP_R, axis=1)
    n_units = kc_ref.shape[0]
    ck = PROMPT_KV_CHUNK
    piece = PROMPT_KV_PIECE
    for g in range(N_KV_A):
        top = jnp.concatenate([qt_ref[(g * GROUP_R + r) * HEAD_DIM:(g * GROUP_R + r + 1) * HEAD_DIM, :]
                               for r in range(GROUP_R)], axis=1)
        zero = jnp.zeros_like(top)
        qg = jnp.concatenate([top, zero] if g == 0 else [zero, top], axis=0)
        rows_g = slice(g * HEAD_DIM, (g + 1) * HEAD_DIM)
        unit = lax.broadcasted_iota(jnp.int32, (n_units, lanes), 0)
        p_c = _softmax_cols(_dot(kc_ref[...], qg), unit * CMP_STRIDE + (CMP_LEN - 1) <= qpos)
        o_c = _dot(vct_ref[rows_g, :], p_c.astype(BF16))
        p_sum = p_c[:, 0:Q_BLOCK]
        for r in range(1, GROUP_R):
            p_sum = p_sum + p_c[:, r * Q_BLOCK:(r + 1) * Q_BLOCK]
        sel_ref[...] = (1.0 - _select_blocks(_dot_01(ovl_ref[...], p_sum), qpos1)) * NEG

        def sel_scores(c):
            bias = sel_ref[pl.ds(pl.multiple_of(c * CHUNK_BLOCKS, CHUNK_BLOCKS), CHUNK_BLOCKS), :]
            bias = jnp.concatenate([bias] * GROUP_R, axis=1).astype(BF16)
            w = jnp.concatenate([qg, bias, jnp.zeros((KV_HALF - CHUNK_BLOCKS, lanes), BF16)], axis=0)
            k0 = pl.multiple_of(c * ck, ck)
            starts = [k0 + h * piece for h in range(ck // piece)]
            s_list = [_dot(ks_ref[pl.ds(k, piece), :], w) for k in starts]
            pv_list = [lambda p, k=k: _dot(vst_ref[rows_g, pl.ds(k, piece)], p) for k in starts]
            return starts, s_list, pv_list

        def sel_step(c, carry):
            _, s_list, pv_list = sel_scores(c)
            return _online_update_biased(carry, s_list, pv_list)

        n_full = start // ck
        carry = (jnp.full((1, lanes), SCORE_FLOOR, F32),) + _online_init(HEAD_DIM, lanes)[1:]
        carry = lax.fori_loop(0, n_full, sel_step, carry)
        starts, s_list, pv_list = sel_scores(n_full)
        row = lax.broadcasted_iota(jnp.int32, (piece, lanes), 0)
        s_list = [jnp.where(k + row <= qpos, s, NEG) for k, s in zip(starts, s_list)]
        o_s = _online_finish(_online_update_biased(carry, s_list, pv_list))
        carry = _online_init(HEAD_DIM, lanes)
        for j in range(WIN_BLOCKS):
            wpos = start - WINDOW + j * Q_BLOCK + lax.broadcasted_iota(jnp.int32, (Q_BLOCK, lanes), 0)
            dpos = qpos - wpos
            mask = (dpos >= 0) & (dpos < WINDOW) & (wpos >= 0)
            vwt = vwt_refs[j]
            carry = _online_update(carry, _dot(kw_refs[j][...], qg), mask, lambda p, vwt=vwt: _dot(vwt[rows_g, :], p))
        o_w = _online_finish(carry)
        for r in range(GROUP_R):
            h = g * GROUP_R + r
            ln = slice(r * Q_BLOCK, (r + 1) * Q_BLOCK)
            o = (gt_ref[h:h + 1, :] * o_c[:, ln] + gt_ref[N_HEADS_A + h:N_HEADS_A + h + 1, :] * o_s[:, ln]
                 + gt_ref[2 * N_HEADS_A + h:2 * N_HEADS_A + h + 1, :] * o_w[:, ln])
            o_ref[h * HEAD_DIM:(h + 1) * HEAD_DIM, :] = o.astype(o_ref.dtype)


def _nsa_prompt(qt, kc, vct, ks, vst, kw, vwt, small_t):
    t = qt.shape[1]
    n_units = kc.shape[0]
    n_blocks = t // SEL_BLOCK
    ovl = _overlap_matrix(n_blocks, n_units)
    assert t % PROMPT_KV_CHUNK == 0
    ks = jnp.concatenate([ks, _block_onehot(t)], axis=1)
    nq = t // Q_BLOCK
    const = lambda a: pl.BlockSpec(a.shape, lambda i: (0, 0))
    back = WINDOW // Q_BLOCK
    kw_specs = [pl.BlockSpec((Q_BLOCK, KV_HALF), lambda i, j=j: (jnp.maximum(i - back + j, 0), 0))
                for j in range(WIN_BLOCKS)]
    vwt_specs = [pl.BlockSpec((KV_HALF, Q_BLOCK), lambda i, j=j: (0, jnp.maximum(i - back + j, 0)))
                 for j in range(WIN_BLOCKS)]
    return pl.pallas_call(
        _nsa_prompt_kernel,
        grid=(nq,),
        in_specs=[pl.BlockSpec((A_WIDTH, Q_BLOCK), lambda i: (0, i)), const(kc), const(vct), const(ovl),
                  const(ks), const(vst)] + kw_specs + vwt_specs
                 + [pl.BlockSpec((SMALL_COLS, Q_BLOCK), lambda i: (0, i))],
        out_specs=pl.BlockSpec((A_WIDTH, Q_BLOCK), lambda i: (0, i)),
        out_shape=jax.ShapeDtypeStruct((A_WIDTH, t), BF16),
        scratch_shapes=[pltpu.VMEM((n_blocks, Q_BLOCK), F32)],
        compiler_params=_params("arbitrary"),
        name="nsa_prompt",
    )(qt, kc, vct, ovl, ks, vst, *([kw] * WIN_BLOCKS), *([vwt] * WIN_BLOCKS), small_t)


NEW_ROWS_PAD = 16


def _split3(x):
    hi = x.astype(BF16)
    r1 = x - hi.astype(F32)
    mid = r1.astype(BF16)
    return hi, mid, (r1 - mid.astype(F32)).astype(BF16)


def _nsa_sample_kernel(pt_ref, pool_ref, qbd_ref, kc_ref, vct_ref, ovl_ref, rep_ref, oh_ref, new_ref, wint_ref,
                       wnew_ref, gl_ref, o_ref, xt, sel_ref, sem, *, t_new):
    slot = _gather_pipeline(pt_ref, pool_ref, xt, sem)
    past = xt.shape[2]
    lanes = LANES
    qbd = qbd_ref[0]
    qpos = past + lax.broadcasted_iota(jnp.int32, (1, lanes), 1) % t_new
    n_units = kc_ref.shape[1]
    unit = lax.broadcasted_iota(jnp.int32, (n_units, lanes), 0)
    p_c = _softmax_cols(_dot(kc_ref[0], qbd), unit * CMP_STRIDE + (CMP_LEN - 1) <= qpos)
    o_c = _dot(vct_ref[0], p_c.astype(BF16))
    imp = _dot_01(ovl_ref[...], p_c)
    imp = sum(_dot(t, rep_ref[...]) for t in _split3(imp))
    sel_ref[...] = (1.0 - _select_blocks(imp, qpos)) * NEG
    ck = PROMPT_KV_CHUNK
    piece = PROMPT_KV_PIECE

    def sel_step(c, carry):
        bias = sel_ref[c * CHUNK_BLOCKS:(c + 1) * CHUNK_BLOCKS, :].astype(BF16)
        w = jnp.concatenate([qbd, bias, jnp.zeros((KV_HALF - CHUNK_BLOCKS, lanes), BF16)], axis=0)
        k0 = c * ck
        s_list, pv_list = [], []
        for h in range(ck // piece):
            k = k0 + h * piece
            kt = xt[slot, 0:KV_HALF, pl.ds(k, piece)].astype(BF16)
            s_list.append(_dot_tn(jnp.concatenate([kt, oh_ref[:, h * piece:(h + 1) * piece]], axis=0), w))
            pv_list.append(lambda p, k=k: _dot(xt[slot, KV_HALF:KV_COLS, pl.ds(k, piece)].astype(BF16), p))
        return _online_update_biased(carry, s_list, pv_list)

    carry = (jnp.full((1, lanes), SCORE_FLOOR, F32),) + _online_init(KV_HALF, lanes)[1:]
    for c in range(past // ck):
        carry = sel_step(c, carry)

    def new_rows(ref):
        x = ref[0]
        pad = jnp.zeros((NEW_ROWS_PAD - t_new, KV_COLS), F32)
        x = jnp.concatenate([x, pad], axis=0).astype(BF16)
        return x[:, 0:KV_HALF], x[:, KV_HALF:KV_COLS]

    row = lax.broadcasted_iota(jnp.int32, (NEW_ROWS_PAD, lanes), 0)
    new_ok = (row < t_new) & (past + row <= qpos)
    k_new, v_new = new_rows(new_ref)
    s_new = _dot(k_new, qbd) + sel_ref[pl.ds(past // SEL_BLOCK, 1), :]
    carry = _online_update_biased(carry, [jnp.where(new_ok, s_new, NEG)], [lambda p: _dot_tn(v_new, p)])
    o_s = _online_finish(carry)
    n_win = wint_ref.shape[2]
    wpos = past - n_win + lax.broadcasted_iota(jnp.int32, (n_win, lanes), 0)
    dpos = qpos - wpos
    w_mask = (dpos >= 0) & (dpos < WINDOW) & (wpos >= 0)
    s_w = _dot_tn(wint_ref[0, 0:KV_HALF, :].astype(BF16), qbd)
    carry = _online_update(_online_init(KV_HALF, lanes), s_w, w_mask,
                           lambda p: _dot(wint_ref[0, KV_HALF:KV_COLS, :].astype(BF16), p))
    k_wn, v_wn = new_rows(wnew_ref)
    carry = _online_update(carry, _dot(k_wn, qbd), new_ok & (qpos - past - row < WINDOW), lambda p: _dot_tn(v_wn, p))
    o_w = _online_finish(carry)
    o_ref[0] = gl_ref[0, 0:1, :] * o_c + gl_ref[0, 1:2, :] * o_s + gl_ref[0, 2:3, :] * o_w


def _nsa_sample(page_table, pool, q, kc, vct, kvs_new, win_state, kvw_new, gates):
    b, n_pages = page_table.shape
    t_new = q.shape[1]
    page_rows = pool.shape[2]
    past = n_pages * page_rows
    n_units = kc.shape[1]
    assert t_new <= SEL_BLOCK and t_new <= NEW_ROWS_PAD and past % PROMPT_KV_CHUNK == 0
    oh = _block_onehot(PROMPT_KV_CHUNK).T
    used = N_KV_A * GROUP_R * t_new
    assert used <= LANES
    qs = q.reshape(b, t_new, N_KV_A, GROUP_R, HEAD_DIM).transpose(0, 2, 4, 3, 1)
    qs = qs.reshape(b, N_KV_A, HEAD_DIM, GROUP_R * t_new)
    qbd = jnp.zeros((b, N_KV_A, HEAD_DIM, N_KV_A, GROUP_R * t_new), BF16)
    for g in range(N_KV_A):
        qbd = qbd.at[:, g, :, g, :].set(qs[:, g])
    qbd = jnp.pad(qbd.reshape(b, KV_HALF, used), ((0, 0), (0, 0), (0, LANES - used)))
    lane = jnp.arange(LANES)
    same = ((lane[:, None] // (GROUP_R * t_new) == lane[None, :] // (GROUP_R * t_new))
            & (lane[:, None] % t_new == lane[None, :] % t_new) & (lane[:, None] < used) & (lane[None, :] < used))
    rep = same.astype(BF16)
    gl = gates.reshape(b, t_new, 3, N_KV_A, GROUP_R).transpose(0, 2, 3, 4, 1).reshape(b, 3, used)
    gl = jnp.pad(gl, ((0, 0), (0, 8 - 3), (0, LANES - used)))
    n_blocks = -(-(past + t_new) // SEL_BLOCK)
    n_blocks_pad = -(-n_blocks // 8) * 8
    ovl = _overlap_matrix(n_blocks_pad, n_units)
    n_slots = min(2, b)
    const = lambda a: pl.BlockSpec(a.shape, lambda i, pt: (0,) * a.ndim)
    per_seq = lambda a: pl.BlockSpec((1,) + a.shape[1:], lambda i, pt: (i,) + (0,) * (a.ndim - 1))
    kvs_new = kvs_new.reshape(b, t_new, KV_COLS)
    kvw_new = kvw_new.reshape(b, t_new, KV_COLS)
    grid_spec = pltpu.PrefetchScalarGridSpec(
        num_scalar_prefetch=1,
        grid=(b,),
        in_specs=[pl.BlockSpec(memory_space=pl.ANY), per_seq(qbd), per_seq(kc), per_seq(vct), const(ovl), const(rep),
                  const(oh), per_seq(kvs_new), per_seq(win_state), per_seq(kvw_new), per_seq(gl)],
        out_specs=pl.BlockSpec((1, KV_HALF, LANES), lambda i, pt: (i, 0, 0)),
        scratch_shapes=[pltpu.VMEM((n_slots, KV_COLS, past), F32), pltpu.VMEM((n_blocks_pad, LANES), F32),
                        pltpu.SemaphoreType.DMA((n_slots,))],
    )
    o = pl.pallas_call(
        functools.partial(_nsa_sample_kernel, t_new=t_new),
        grid_spec=grid_spec,
        out_shape=jax.ShapeDtypeStruct((b, KV_HALF, LANES), F32),
        compiler_params=_params("arbitrary"),
        name="nsa_sample",
    )(page_table, pool, qbd, kc, vct, ovl, rep, oh, kvs_new, win_state, kvw_new, gl)
    o = o[:, :, :used].reshape(b, N_KV_A, HEAD_DIM, N_KV_A, GROUP_R, t_new)
    o = jnp.stack([o[:, g, :, g] for g in range(N_KV_A)], axis=1)
    return o.transpose(0, 4, 1, 3, 2).reshape(b * t_new, A_WIDTH)


CONV_PAD = 8


def _log_sigmoid(x):
    return jnp.minimum(x, 0.0) - jnp.log(1.0 + jnp.exp(-jnp.abs(x)))


def _mlstm_kernel(u_ref, vm_ref, om_ref, sm_ref, ift_ref, cw_ref, cb_ref, wq_ref, wk_ref, mg_ref,
                  c0_ref, n0_ref, m0_ref, conv0_ref, hm_ref, c_out, n_out, m_out, conv_out,
                  c_s, n_s, m_s, ext_s):
    c = pl.program_id(1)
    chunk = u_ref.shape[0]
    keep = CONV_W - 1

    @pl.when(c == 0)
    def _():
        c_s[...] = c0_ref[0]
        n_s[...] = n0_ref[0]
        m_s[...] = m0_ref[0]
        ext_s[0:CONV_PAD, :] = conv0_ref[0]

    ext_s[CONV_PAD:CONV_PAD + chunk, :] = u_ref[...]
    conv = cb_ref[...]
    for j in range(CONV_W):
        conv = conv + ext_s[pl.ds(CONV_PAD - keep + j, chunk), :] * cw_ref[j:j + 1, :]
    ext_s[CONV_PAD - keep:CONV_PAD, :] = ext_s[CONV_PAD + chunk - keep:CONV_PAD + chunk, :]
    xc = (conv * jax.nn.sigmoid(conv)).astype(BF16)
    ti = lax.broadcasted_iota(jnp.int32, (chunk, chunk), 0)
    si = lax.broadcasted_iota(jnp.int32, (chunk, chunk), 1)
    causal = si <= ti
    head_lane = lax.broadcasted_iota(jnp.int32, (1, LANES), 1)
    sm = sm_ref[...]
    m_all = m_s[...]
    for h in range(N_HEADS_M):
        cols = slice(h * HEAD_DIM_M, (h + 1) * HEAD_DIM_M)
        xh = xc[:, cols]
        qm = _dot(xh, wq_ref[h])
        km = _dot(xh, wk_ref[h]) * HEAD_DIM_M ** -0.5
        qb, kb = qm.astype(BF16), km.astype(BF16)
        vb = vm_ref[:, cols].astype(BF16)
        ig_col = sm[:, SM_IG + h:SM_IG + h + 1]
        lf_col = _log_sigmoid(sm[:, SM_FG + h:SM_FG + h + 1])
        ig_row = ift_ref[0, 0, h:h + 1, :]
        lf_row = _log_sigmoid(ift_ref[0, 0, N_HEADS_M + h:N_HEADS_M + h + 1, :])
        fcum_col = jnp.sum(jnp.where(causal, lf_row, 0.0), axis=1, keepdims=True)
        fcum_row = jnp.sum(jnp.where(ti <= si, lf_col, 0.0), axis=0, keepdims=True)
        d_row = ig_row - fcum_row
        cmax_col = jnp.max(jnp.where(causal, d_row, NEG), axis=1, keepdims=True)
        m_prev = m_all[:, h:h + 1]
        m_row = fcum_col + jnp.maximum(m_prev, cmax_col)
        w_intra = jnp.exp(jnp.where(causal, (fcum_col - m_row) + d_row, NEG))
        w_inter = jnp.exp(m_prev + fcum_col - m_row)
        c_prev = c_s[h]
        n_prev = n_s[h:h + 1, :]
        qk = _dot_nt(qb, kb) * w_intra
        num = w_inter * _dot(qb, c_prev.astype(BF16)) + _dot(qk.astype(BF16), vb)
        den = w_inter * jnp.sum(qm * n_prev, axis=1, keepdims=True) + jnp.sum(qk, axis=1, keepdims=True)
        hh = num / jnp.maximum(jnp.abs(den), jnp.exp(-m_row))
        m_new = m_row[chunk - 1:chunk, :]
        f_last = fcum_col[chunk - 1:chunk, :]
        w_keep = jnp.exp(m_prev + f_last - m_new)
        w_src = jnp.exp(ig_col + f_last - fcum_col - m_new)
        ks = km * w_src
        c_s[h] = w_keep * c_prev + _dot_tn(ks.astype(BF16), vb)
        n_s[h:h + 1, :] = w_keep * n_prev + jnp.sum(ks, axis=0, keepdims=True)
        m_all = jnp.where(head_lane == h, m_new, m_all)
        y = hh * lax.rsqrt(jnp.mean(hh * hh, axis=1, keepdims=True) + EPS) * mg_ref[:, cols]
        hm_ref[:, cols] = y * jax.nn.sigmoid(om_ref[:, cols])
    m_s[...] = m_all

    @pl.when(c == pl.num_programs(1) - 1)
    def _():
        c_out[0] = c_s[...]
        n_out[0] = n_s[...]
        m_out[0] = m_s[...]
        conv_out[0] = ext_s[0:CONV_PAD, :]


def _mlstm(u, vm, om, small, conv_w, conv_b, w_mq, w_mk, mnorm_g, c0, n0, m0, conv0, b, t, chunk):
    n_chunks = t // chunk
    ift = small[:, SM_IG:SM_IG + 2 * N_HEADS_M].reshape(b, n_chunks, chunk, 2 * N_HEADS_M).transpose(0, 1, 3, 2)
    m0p = jnp.pad(m0.reshape(b, 1, N_HEADS_M), ((0, 0), (0, 0), (0, LANES - N_HEADS_M)))
    conv0p = jnp.pad(conv0, ((0, 0), (CONV_PAD - (CONV_W - 1), 0), (0, 0)))
    row = lambda w: pl.BlockSpec((chunk, w), lambda i, c: (i * n_chunks + c, 0))
    const = lambda a: pl.BlockSpec(a.shape, lambda i, c: (0,) * a.ndim)
    per_seq = lambda a: pl.BlockSpec((1,) + a.shape[1:], lambda i, c: (i,) + (0,) * (a.ndim - 1))
    wq = w_mq.astype(BF16)
    wk = w_mk.astype(BF16)
    cb = conv_b.reshape(1, M_WIDTH)
    mg = mnorm_g.reshape(1, M_WIDTH)
    out_shape = (jax.ShapeDtypeStruct((b * t, M_WIDTH), F32),
                 jax.ShapeDtypeStruct(c0.shape, F32), jax.ShapeDtypeStruct(n0.shape, F32),
                 jax.ShapeDtypeStruct(m0p.shape, F32), jax.ShapeDtypeStruct(conv0p.shape, F32))
    hm, c_new, n_new, m_new, conv_new = pl.pallas_call(
        _mlstm_kernel,
        grid=(b, n_chunks),
        in_specs=[row(M_WIDTH), row(M_WIDTH), row(M_WIDTH), row(SMALL_COLS),
                  pl.BlockSpec((1, 1, 2 * N_HEADS_M, chunk), lambda i, c: (i, c, 0, 0)),
                  const(conv_w), const(cb), const(wq), const(wk), const(mg),
                  per_seq(c0), per_seq(n0), per_seq(m0p), per_seq(conv0p)],
        out_specs=(row(M_WIDTH), per_seq(c0), per_seq(n0), per_seq(m0p), per_seq(conv0p)),
        out_shape=out_shape,
        scratch_shapes=[pltpu.VMEM(c0.shape[1:], F32), pltpu.VMEM(n0.shape[1:], F32), pltpu.VMEM((1, LANES), F32),
                        pltpu.VMEM((CONV_PAD + chunk, M_WIDTH), F32)],
        compiler_params=_params("arbitrary", "arbitrary"),
        name="mlstm",
    )(u, vm, om, small, ift, conv_w, cb, wq, wk, mg, c0, n0, m0p, conv0p)
    return hm, c_new, n_new, m_new[:, 0, :N_HEADS_M], conv_new[:, CONV_PAD - (CONV_W - 1):, :]


def _outproj_kernel(x_ref, oa_ref, hm_ref, wo_ref, g1_ref, ng_ref, sc_ref, sh_ref, wq_ref,
                    x1_ref, h2_ref, pq_ref, *, oa_transposed):
    oa = oa_ref[...].astype(BF16)
    wo_a = wo_ref[0:A_WIDTH, :]
    mix = _dot_tn(oa, wo_a) if oa_transposed else _dot(oa, wo_a)
    mix = mix + _dot(hm_ref[...].astype(BF16), wo_ref[A_WIDTH:A_WIDTH + M_WIDTH, :])
    x1 = x_ref[...] + g1_ref[...] * mix
    x1_ref[...] = x1
    y = x1 * lax.rsqrt(jnp.mean(x1 * x1, axis=-1, keepdims=True) + EPS) * ng_ref[...]
    h2 = (y * (1.0 + sc_ref[...]) + sh_ref[...]).astype(BF16)
    h2_ref[...] = h2
    pq_ref[...] = _dot(h2, wq_ref[...]).astype(BF16)


def _outproj(x, oa, hm, w_out, g1, norm_g, sc, sh, peer_wq, tm, oa_transposed):
    r, d = x.shape
    per_row = g1.shape[0] != 1
    mod = pl.BlockSpec((tm, d), lambda i: (i, 0)) if per_row else pl.BlockSpec((1, d), lambda i: (0, 0))
    row = lambda w: pl.BlockSpec((tm, w), lambda i: (i, 0))
    const = lambda a: pl.BlockSpec(a.shape, lambda i: (0, 0))
    oa_spec = pl.BlockSpec((A_WIDTH, tm), lambda i: (0, i)) if oa_transposed else row(A_WIDTH)
    nq = peer_wq.shape[1]
    return pl.pallas_call(
        functools.partial(_outproj_kernel, oa_transposed=oa_transposed),
        grid=(r // tm,),
        in_specs=[row(d), oa_spec, row(M_WIDTH), const(w_out), mod, const(norm_g), mod, mod, const(peer_wq)],
        out_specs=(row(d), row(d), row(nq)),
        out_shape=(jax.ShapeDtypeStruct((r, d), F32), jax.ShapeDtypeStruct((r, d), BF16),
                   jax.ShapeDtypeStruct((r, nq), BF16)),
        compiler_params=_params("arbitrary"),
        name="outproj",
    )(x, oa, hm, w_out, g1, norm_g, sc, sh, peer_wq)


PEER_CHUNK_I = 8
PEER_HALF = PEER_DKEY // 2


SUBLANES = 8


def _oddeven_merge(lo, hi, r):
    step = r * 2
    if step < hi - lo:
        yield from _oddeven_merge(lo, hi, step)
        yield from _oddeven_merge(lo + r, hi, step)
        yield from [(i, i + r) for i in range(lo + r, hi - r, step)]
    else:
        yield (lo, lo + r)


def _oddeven_sort(lo, hi):
    if hi - lo >= 1:
        mid = lo + (hi - lo) // 2
        yield from _oddeven_sort(lo, mid)
        yield from _oddeven_sort(mid + 1, hi)
        yield from _oddeven_merge(lo, hi, 1)


def _top_values(x):
    k = PEER_TOPK
    r = [x[v * SUBLANES:(v + 1) * SUBLANES] for v in range(k)]

    def exchange(t, i, j):
        t[i], t[j] = jnp.maximum(t[i], t[j]), jnp.minimum(t[i], t[j])

    for i, j in _oddeven_sort(0, k - 1):
        exchange(r, i, j)
    shift = SUBLANES // 2
    while shift >= 1:
        r = [jnp.maximum(r[v], pltpu.roll(r[k - 1 - v], shift, 0)) for v in range(k)]
        d = k // 2
        while d >= 1:
            for v in range(k):
                if v & d == 0:
                    exchange(r, v, v + d)
            d //= 2
        shift //= 2
    return jnp.concatenate([t[0:1] for t in r], axis=0)


def _peer_kernel(pq_ref, keys_ref, h2t_ref, u_ref, vt_ref, x1_ref, g2_ref, ng_ref, y_ref,
                 s_s, top_s, thr_s, e1_s, e2_s, act_s, w_s, acc_s):
    j = pl.program_id(1)
    tt = pq_ref.shape[0]
    k = PEER_TOPK

    @pl.when(j == 0)
    def _():
        def score(hc, _):
            col = pl.multiple_of(hc * PEER_HALF, PEER_HALF)
            s = _dot_nt(keys_ref[hc], pq_ref[:, pl.ds(col, PEER_HALF)])
            s_s[hc] = s
            top_s[hc] = _top_values(s)
            return 0
        lax.fori_loop(0, 2 * PEER_HEADS, score, 0)

        def candidates(x, y):
            pieces = [x[0:1] * y]
            pieces += [x[i:i + 1] * y[0:k // 2] for i in range(1, k // 2)]
            pieces += [x[k // 2:k] * y[0:1]]
            n_rows = sum(p.shape[0] for p in pieces)
            pieces += [jnp.full((k * SUBLANES - n_rows, tt), -1.0, F32)]
            return jnp.concatenate(pieces, axis=0)

        def head(h, _):
            a = top_s[2 * h]
            b = top_s[2 * h + 1]
            ea = jnp.exp(a - a[0:1])
            eb = jnp.exp(b - b[0:1])
            z = jnp.sum(_top_values(candidates(ea, eb)), axis=0, keepdims=True)
            thr = _top_values(candidates(ea, eb / z))[k - 1:k]
            thr_s[h] = jnp.broadcast_to(thr, thr_s.shape[1:])
            e1_s[h] = jnp.exp(s_s[2 * h] - a[0:1])
            e2_s[h] = jnp.exp(s_s[2 * h + 1] - b[0:1]) / z
            return 0
        lax.fori_loop(0, PEER_HEADS, head, 0)
        acc_s[...] = jnp.zeros(acc_s.shape, F32)

    act_s[...] = jax.nn.gelu(_dot(u_ref[...], h2t_ref[...]), approximate=True)

    def build(ii, _):
        i = j * PEER_CHUNK_I + ii
        r = pl.ds(pl.multiple_of(ii * PEER_NKEYS, PEER_NKEYS), PEER_NKEYS)
        e1_rows = [e1_s[h, pl.ds(i, 1), :] for h in range(PEER_HEADS)]
        for lt in range(tt // LANES):
            cols = slice(lt * LANES, (lt + 1) * LANES)
            g = jnp.zeros((PEER_NKEYS, LANES), F32)
            for h in range(PEER_HEADS):
                val = e2_s[h, :, cols] * e1_rows[h][:, cols]
                g = g + jnp.where(val >= thr_s[h, 0:1, cols], val, 0.0)
            w_s[r, cols] = (g * act_s[r, cols]).astype(BF16)
        return 0
    lax.fori_loop(0, PEER_CHUNK_I, build, 0)
    acc_s[...] += _dot(vt_ref[...], w_s[...])

    @pl.when(j == pl.num_programs(1) - 1)
    def _():
        x2 = x1_ref[...] + g2_ref[...] * acc_s[...].T
        y_ref[...] = x2 * lax.rsqrt(jnp.mean(x2 * x2, axis=-1, keepdims=True) + EPS) * ng_ref[...]


def _peer(pq, keys, h2t, u_tab, vt_tab, x1, g2, norm_g, tt):
    n_tok, d = x1.shape
    n_exp = u_tab.shape[0]
    ec = PEER_CHUNK_I * PEER_NKEYS
    per_row = g2.shape[0] != 1
    mod = pl.BlockSpec((tt, d), lambda t, j: (t, 0)) if per_row else pl.BlockSpec((1, d), lambda t, j: (0, 0))
    f32_tok = lambda n: pltpu.VMEM((n, PEER_NKEYS, tt), F32)
    return pl.pallas_call(
        _peer_kernel,
        grid=(n_tok // tt, n_exp // ec),
        in_specs=[pl.BlockSpec((tt, pq.shape[1]), lambda t, j: (t, 0)),
                  pl.BlockSpec(keys.shape, lambda t, j: (0, 0, 0)),
                  pl.BlockSpec((d, tt), lambda t, j: (0, t)),
                  pl.BlockSpec((ec, d), lambda t, j: (j, 0)),
                  pl.BlockSpec((d, ec), lambda t, j: (0, j)),
                  pl.BlockSpec((tt, d), lambda t, j: (t, 0)), mod,
                  pl.BlockSpec((1, d), lambda t, j: (0, 0))],
        out_specs=pl.BlockSpec((tt, d), lambda t, j: (t, 0)),
        out_shape=jax.ShapeDtypeStruct((n_tok, d), F32),
        scratch_shapes=[f32_tok(2 * PEER_HEADS), pltpu.VMEM((2 * PEER_HEADS, PEER_TOPK, tt), F32),
                        pltpu.VMEM((PEER_HEADS, 8, tt), F32), f32_tok(PEER_HEADS), f32_tok(PEER_HEADS),
                        pltpu.VMEM((ec, tt), F32), pltpu.VMEM((ec, tt), BF16), pltpu.VMEM((d, tt), F32)],
        compiler_params=_params("arbitrary", "arbitrary"),
        name="peer",
    )(pq, keys, h2t, u_tab, vt_tab, x1, g2, norm_g)


def kernel(x_prompt, x_sample, cache_cmp_kv, cache_slc_kv, page_table, state_win_kv, state_C, state_n, state_m, state_conv, c_prompt, c_sample, w_ada, b_ada, norm1_g, w_in, b_gate_a, cmp_pe_k, cmp_w1_k, cmp_w2_k, cmp_pe_v, cmp_w1_v, cmp_w2_v, conv_w, conv_b, w_mq, w_mk, b_i, b_f, mnorm_g, w_out, norm2_g, peer_wq, peer_keys, peer_u, peer_v, normf_g):
    b_p, t_p, dm = x_prompt.shape
    b_s, t_s, _ = x_sample.shape
    page_rows = cache_cmp_kv.shape[1]
    past_len = page_table.shape[1] * page_rows
    kv_shape = (2, N_KV_A, HEAD_DIM)

    c_all = jnp.concatenate([c_prompt, c_sample], axis=0)
    ada = _ada(jnp.pad(c_all, ((0, -c_all.shape[0] % 8), (0, 0))), w_ada, b_ada)
    w_in_p, bias_small = _permute_w_in(w_in, b_gate_a, b_i, b_f)
    cw = _compress_weights(cmp_pe_k, cmp_w1_k, cmp_w2_k, cmp_pe_v, cmp_w1_v, cmp_w2_v)
    w_out_b = w_out.astype(BF16)
    wq_b = peer_wq.astype(BF16)
    keys_b = peer_keys.reshape(2 * PEER_HEADS, PEER_NKEYS, PEER_HALF).astype(BF16)
    u_b = peer_u.astype(BF16)
    vt_b = peer_v.T.astype(BF16)
    row_vec = lambda v: v.reshape(1, -1)

    def modulation(rows, t):
        parts = [rows[:, k * dm:(k + 1) * dm] for k in range(6)]
        return parts if rows.shape[0] == 1 else [jnp.repeat(p, t, axis=0) for p in parts]

    def tile_rows(n):
        return min(256, n)

    def peer_stage(pq, h2, x1, g2):
        n = x1.shape[0]
        tt = min(512, -(-n // LANES) * LANES)
        pad = -n % tt
        padr = lambda a: jnp.pad(a, ((0, pad), (0, 0)))
        g2p = g2 if g2.shape[0] == 1 else padr(g2)
        y = _peer(padr(pq), keys_b, padr(h2).T, u_b, vt_b, padr(x1), g2p, row_vec(normf_g), tt)
        return y[:n]

    sh1, sc1, g1, sh2, sc2, g2 = modulation(ada[0:b_p], t_p)
    xp = x_prompt.reshape(b_p * t_p, dm)
    pos_p = jnp.tile(jnp.arange(t_p), b_p)
    q, kvc_p, kvs_p, kvw_p, u, vm, om, sm = _inproj(xp, row_vec(norm1_g), sc1, sh1, w_in_p, bias_small,
                                                    _rot_tables(pos_p), tile_rows(b_p * t_p))
    table = jnp.arange(t_p // page_rows, dtype=jnp.int32).reshape(1, -1)
    sm_t = sm.T
    oa_t = []
    for s in range(b_p):
        rows = slice(s * t_p, (s + 1) * t_p)
        pool_p = kvc_p[rows].reshape(t_p // page_rows, page_rows, KV_COLS).transpose(0, 2, 1)
        kc, vct = _compress(table, pool_p, cw)
        ks, kw = kvs_p[rows], kvw_p[rows]
        oa_t.append(_nsa_prompt(q[rows].T, kc[0], vct[0], ks[:, :KV_HALF].astype(BF16),
                                ks[:, KV_HALF:].astype(BF16).T, kw[:, :KV_HALF].astype(BF16),
                                kw[:, KV_HALF:].astype(BF16).T, sm_t[:, rows]))
    oa_t = oa_t[0] if b_p == 1 else jnp.concatenate(oa_t, axis=1)
    hm, c_p, n_p, m_p, conv_p = _mlstm(
        u, vm, om, sm, conv_w, conv_b, w_mq, w_mk, mnorm_g,
        jnp.zeros((b_p, N_HEADS_M, HEAD_DIM_M, HEAD_DIM_M), F32), jnp.zeros((b_p, N_HEADS_M, HEAD_DIM_M), F32),
        jnp.zeros((b_p, N_HEADS_M), F32), jnp.zeros((b_p, CONV_W - 1, M_WIDTH), F32), b_p, t_p, min(128, t_p))
    x1, h2, pq = _outproj(xp, oa_t, hm, w_out_b, g1, row_vec(norm2_g), sc2, sh2, wq_b, tile_rows(b_p * t_p), True)
    y_p = peer_stage(pq, h2, x1, g2).reshape(b_p, t_p, dm)

    sh1, sc1, g1, sh2, sc2, g2 = modulation(ada[b_p:b_p + b_s], t_s)
    xs = x_sample.reshape(b_s * t_s, dm)
    pos_s = jnp.tile(past_len + jnp.arange(t_s), b_s)
    q, kvc_s, kvs_s, kvw_s, u, vm, om, sm = _inproj(xs, row_vec(norm1_g), sc1, sh1, w_in_p, bias_small,
                                                    _rot_tables(pos_s), tile_rows(b_s * t_s))
    kc, vct = _compress(page_table, _transposed_pool(cache_cmp_kv), cw)
    n_win = state_win_kv.shape[1]
    oa = _nsa_sample(page_table, _transposed_pool(cache_slc_kv), q.reshape(b_s, t_s, A_WIDTH),
                     kc, vct, kvs_s, _transposed_pool(state_win_kv), kvw_s,
                     sm[:, 0:GATE_COLS].reshape(b_s, t_s, GATE_COLS))
    hm, c_s, n_s, m_s, conv_s = _mlstm(u, vm, om, sm, conv_w, conv_b, w_mq, w_mk, mnorm_g, state_C.astype(F32),
                                       state_n.astype(F32), state_m.astype(F32), state_conv.astype(F32),
                                       b_s, t_s, t_s)
    x1, h2, pq = _outproj(xs, oa, hm, w_out_b, g1, row_vec(norm2_g), sc2, sh2, wq_b, tile_rows(b_s * t_s), False)
    y_s = peer_stage(pq, h2, x1, g2).reshape(b_s, t_s, dm)

    kv5 = lambda a, b, t: a.reshape((b, t) + kv_shape)
    kvw_p5, kvw_s5 = kv5(kvw_p, b_p, t_p), kv5(kvw_s, b_s, t_s)
    win_p = jnp.concatenate([jnp.zeros((b_p, n_win) + kv_shape, F32), kvw_p5], axis=1)[:, -n_win:]
    win_s = jnp.concatenate([state_win_kv.astype(F32), kvw_s5], axis=1)[:, -n_win:]
    return (y_p, y_s, kv5(kvc_p, b_p, t_p), kv5(kvc_s, b_s, t_s), kv5(kvs_p, b_p, t_p), kv5(kvs_s, b_s, t_s),
            win_p, win_s, c_p, c_s, n_p, n_s, m_p, m_s, conv_p, conv_s)
```

```python
import functools

import jax
import jax.numpy as jnp
from jax import lax
from jax.experimental import pallas as pl
from jax.experimental.pallas import tpu as pltpu

F32 = jnp.float32
BF16 = jnp.bfloat16

N_HEADS_A = 8
N_KV_A = 2
HEAD_DIM = 64
GROUP_R = N_HEADS_A // N_KV_A
ROT_DIM = HEAD_DIM // 4
ROPE_THETA = 500000.0
CMP_LEN = 32
CMP_STRIDE = 16
CMP_HID = 256
SEL_BLOCK = 64
N_SEL = 16
WINDOW = 512
Q_BLOCK = 128
FORCE_BONUS = 1.0e3
N_HEADS_M = 4
HEAD_DIM_M = 128
CONV_W = 4
PEER_HEADS = 8
PEER_NKEYS = 128
PEER_TOPK = 16
PEER_DKEY = 256
EPS = 1e-6
NEG = -1e30
LOG2_E = 1.4426950408889634

A_WIDTH = N_HEADS_A * HEAD_DIM
M_WIDTH = N_HEADS_M * HEAD_DIM_M
KV_COLS = 2 * N_KV_A * HEAD_DIM
GATE_COLS = 3 * N_HEADS_A
KV_HALF = N_KV_A * HEAD_DIM
LANES = 128
SMALL_COLS = LANES
VMEM_LIMIT = 56 * 1024 * 1024


def _dot(a, b):
    return jnp.dot(a, b, preferred_element_type=F32)


def _dot_nt(a, b):
    return lax.dot_general(a, b, (((1,), (1,)), ((), ())), preferred_element_type=F32)


def _dot_tn(a, b):
    return lax.dot_general(a, b, (((0,), (0,)), ((), ())), preferred_element_type=F32)


def _params(*sem):
    return pltpu.CompilerParams(dimension_semantics=sem, vmem_limit_bytes=VMEM_LIMIT)


def _ada_kernel(c_ref, w_ref, b_ref, o_ref):
    c = c_ref[...]
    s = (c * jax.nn.sigmoid(c)).astype(BF16)
    o_ref[...] = _dot(s, w_ref[...].astype(BF16)) + b_ref[...]


def _ada(c, w_ada, b_ada):
    m, d = c.shape
    n = w_ada.shape[1]
    tn = n // 6
    return pl.pallas_call(
        _ada_kernel,
        grid=(n // tn,),
        in_specs=[pl.BlockSpec((m, d), lambda j: (0, 0)),
                  pl.BlockSpec((d, tn), lambda j: (0, j)),
                  pl.BlockSpec((1, tn), lambda j: (0, j))],
        out_specs=pl.BlockSpec((m, tn), lambda j: (0, j)),
        out_shape=jax.ShapeDtypeStruct((m, n), F32),
        compiler_params=_params("arbitrary"),
        name="ada",
    )(c, w_ada, b_ada.reshape(1, n))


IN_OFF_Q = 0
IN_OFF_KVC = A_WIDTH
IN_OFF_KVS = IN_OFF_KVC + KV_COLS
IN_OFF_KVW = IN_OFF_KVS + KV_COLS
IN_OFF_U = IN_OFF_KVW + KV_COLS
IN_OFF_VM = IN_OFF_U + M_WIDTH
IN_OFF_OM = IN_OFF_VM + M_WIDTH
IN_OFF_SMALL = IN_OFF_OM + M_WIDTH
IN_COLS_PADDED = IN_OFF_SMALL + SMALL_COLS
SM_IG = GATE_COLS
SM_FG = GATE_COLS + N_HEADS_M


def _permute_w_in(w_in, b_gate_a, b_i, b_f):
    d = w_in.shape[0]
    o = 0
    parts = {}
    for name, width in (("q", A_WIDTH), ("kvc", KV_COLS), ("kvs", KV_COLS), ("kvw", KV_COLS), ("g", GATE_COLS),
                        ("u", M_WIDTH), ("vm", M_WIDTH), ("om", M_WIDTH), ("i", N_HEADS_M), ("f", N_HEADS_M)):
        parts[name] = w_in[:, o:o + width]
        o += width
    pad = jnp.zeros((d, SMALL_COLS - GATE_COLS - 2 * N_HEADS_M), w_in.dtype)
    w = jnp.concatenate([parts[k] for k in ("q", "kvc", "kvs", "kvw", "u", "vm", "om", "g", "i", "f")] + [pad], axis=1)
    bias = jnp.concatenate([b_gate_a, b_i, b_f, jnp.zeros((SMALL_COLS - GATE_COLS - 2 * N_HEADS_M,), F32)])
    return w.astype(BF16), bias.reshape(1, SMALL_COLS)


def _rot_tables(pos):
    t = pos.shape[0]
    inv = ROPE_THETA ** (-jnp.arange(0, ROT_DIM, 2, dtype=F32) / ROT_DIM)
    ang = pos.astype(F32)[:, None] * inv[None, :]
    cos, sin = jnp.cos(ang), jnp.sin(ang)
    half = ROT_DIM // 2
    ones = jnp.ones((t, HEAD_DIM - ROT_DIM), F32)
    zeros = jnp.zeros((t, HEAD_DIM - ROT_DIM), F32)
    zh = jnp.zeros((t, half), F32)
    c = jnp.concatenate([cos, cos, ones], axis=1)
    s_lo = jnp.concatenate([-sin, zh, zeros], axis=1)
    s_hi = jnp.concatenate([zh, sin, zeros], axis=1)
    rep = LANES // HEAD_DIM
    return jnp.tile(c, (1, rep)), jnp.tile(s_lo, (1, rep)), jnp.tile(s_hi, (1, rep))


def _inproj_kernel(x_ref, g_ref, sc_ref, sh_ref, w_ref, bsm_ref, rc_ref, rlo_ref, rhi_ref,
                   q_ref, kvc_ref, kvs_ref, kvw_ref, u_ref, vm_ref, om_ref, sm_ref):
    x = x_ref[...]
    y = x * lax.rsqrt(jnp.mean(x * x, axis=-1, keepdims=True) + EPS) * g_ref[...]
    hb = (y * (1.0 + sc_ref[...]) + sh_ref[...]).astype(BF16)
    rc, rlo, rhi = rc_ref[...], rlo_ref[...], rhi_ref[...]
    half = ROT_DIM // 2

    def rot(z):
        return z * rc + pltpu.roll(z, LANES - half, 1) * rlo + pltpu.roll(z, half, 1) * rhi

    zq = _dot(hb, w_ref[:, IN_OFF_Q:IN_OFF_Q + A_WIDTH])
    scale = HEAD_DIM ** -0.5 * LOG2_E
    for c in range(A_WIDTH // LANES):
        q_ref[:, c * LANES:(c + 1) * LANES] = (rot(zq[:, c * LANES:(c + 1) * LANES]) * scale).astype(BF16)
    for ref, off in ((kvc_ref, IN_OFF_KVC), (kvs_ref, IN_OFF_KVS), (kvw_ref, IN_OFF_KVW)):
        z = _dot(hb, w_ref[:, off:off + KV_COLS])
        ref[:, 0:KV_HALF] = rot(z[:, 0:KV_HALF])
        ref[:, KV_HALF:KV_COLS] = z[:, KV_HALF:KV_COLS]
    u_ref[...] = _dot(hb, w_ref[:, IN_OFF_U:IN_OFF_U + M_WIDTH])
    vm_ref[...] = _dot(hb, w_ref[:, IN_OFF_VM:IN_OFF_VM + M_WIDTH])
    om_ref[...] = _dot(hb, w_ref[:, IN_OFF_OM:IN_OFF_OM + M_WIDTH])
    zs = _dot(hb, w_ref[:, IN_OFF_SMALL:IN_OFF_SMALL + SMALL_COLS]) + bsm_ref[...]
    lane = lax.broadcasted_iota(jnp.int32, zs.shape, 1)
    sm_ref[...] = jnp.where(lane < GATE_COLS, jax.nn.sigmoid(zs), zs)


def _inproj(x, norm_g, sc, sh, w_in_p, bias_small, rot_tabs, tm):
    r, d = x.shape
    per_row = sc.shape[0] != 1
    mod_spec = pl.BlockSpec((tm, d), lambda i: (i, 0)) if per_row else pl.BlockSpec((1, d), lambda i: (0, 0))
    row = lambda w: pl.BlockSpec((tm, w), lambda i: (i, 0))
    const = lambda a: pl.BlockSpec(a.shape, lambda i: (0, 0))
    out_shape = (jax.ShapeDtypeStruct((r, A_WIDTH), BF16),
                 jax.ShapeDtypeStruct((r, KV_COLS), F32), jax.ShapeDtypeStruct((r, KV_COLS), F32),
                 jax.ShapeDtypeStruct((r, KV_COLS), F32),
                 jax.ShapeDtypeStruct((r, M_WIDTH), F32), jax.ShapeDtypeStruct((r, M_WIDTH), F32),
                 jax.ShapeDtypeStruct((r, M_WIDTH), F32), jax.ShapeDtypeStruct((r, SMALL_COLS), F32))
    return pl.pallas_call(
        _inproj_kernel,
        grid=(r // tm,),
        in_specs=[row(d), const(norm_g), mod_spec, mod_spec, const(w_in_p), const(bias_small),
                  row(LANES), row(LANES), row(LANES)],
        out_specs=(row(A_WIDTH), row(KV_COLS), row(KV_COLS), row(KV_COLS), row(M_WIDTH), row(M_WIDTH),
                   row(M_WIDTH), row(SMALL_COLS)),
        out_shape=out_shape,
        compiler_params=_params("arbitrary"),
        name="inproj",
    )(x, norm_g, sc, sh, w_in_p, bias_small, *rot_tabs)


def _transposed_pool(rows5):
    n, r = rows5.shape[0], rows5.shape[1]
    return rows5.transpose(0, 2, 3, 4, 1).reshape(n, KV_COLS, r)


GATHER_UNROLL = 8


def _page_copy(pt_ref, pool_ref, xt, sem, seq, slot, p):
    page_rows = pool_ref.shape[2]
    dst = pl.ds(pl.multiple_of(p * page_rows, page_rows), page_rows)
    return pltpu.make_async_copy(pool_ref.at[pt_ref[seq, p]], xt.at[slot, :, dst], sem.at[slot])


def _gather_start(pt_ref, pool_ref, xt, sem, seq, slot):
    def body(p, _):
        _page_copy(pt_ref, pool_ref, xt, sem, seq, slot, p).start()
        return 0
    lax.fori_loop(0, pt_ref.shape[1], body, 0, unroll=GATHER_UNROLL)


def _gather_wait(pt_ref, pool_ref, xt, sem, seq, slot):
    def body(p, _):
        _page_copy(pt_ref, pool_ref, xt, sem, seq, slot, p).wait()
        return 0
    lax.fori_loop(0, pt_ref.shape[1], body, 0, unroll=GATHER_UNROLL)


def _gather_pipeline(pt_ref, pool_ref, xt, sem):
    b = pl.program_id(0)
    nb = pl.num_programs(0)
    n_slots = xt.shape[0]
    slot = b % n_slots

    @pl.when(b == 0)
    def _():
        _gather_start(pt_ref, pool_ref, xt, sem, 0, 0)

    if n_slots > 1:
        @pl.when(b + 1 < nb)
        def _():
            _gather_start(pt_ref, pool_ref, xt, sem, b + 1, 1 - slot)

    _gather_wait(pt_ref, pool_ref, xt, sem, b, slot)
    return slot


CMP_UNIT = CMP_STRIDE
CMP_SLAB = 256


def _compress_weights(pe_k, w1_k, w2_k, pe_v, w1_v, w2_v):
    eye = jnp.eye(N_KV_A, dtype=F32)

    def split_w1(w1):
        w = w1.reshape(CMP_LEN, HEAD_DIM, CMP_HID)

        def bd(part):
            return jnp.einsum('ldh,gk->lgdkh', part, eye).reshape(CMP_UNIT * KV_HALF, N_KV_A * CMP_HID).astype(BF16)
        return bd(w[:CMP_UNIT]), bd(w[CMP_UNIT:])

    def bd_w2(w2):
        return jnp.einsum('hd,gk->ghkd', w2, eye).reshape(N_KV_A * CMP_HID, KV_HALF)

    wka, wkb = split_w1(w1_k)
    wva, wvb = split_w1(w1_v)
    pe_a = jnp.concatenate([jnp.tile(pe_k[:CMP_UNIT], (1, N_KV_A)), jnp.tile(pe_v[:CMP_UNIT], (1, N_KV_A))], axis=1)
    pe_b = jnp.concatenate([jnp.tile(pe_k[CMP_UNIT:], (1, N_KV_A)), jnp.tile(pe_v[CMP_UNIT:], (1, N_KV_A))], axis=1)
    return (pe_a, pe_b, wka, wkb, wva, wvb, bd_w2(w2_k).astype(BF16), bd_w2(w2_v).T.astype(BF16))


def _compress_kernel(pt_ref, pool_ref, pea_ref, peb_ref, wka_ref, wkb_ref, wva_ref, wvb_ref, w2k_ref, w2vt_ref,
                     kc_ref, vct_ref, xt, kbuf, vbuf, a_k, b_k, a_v, b_v, sem):
    slot = _gather_pipeline(pt_ref, pool_ref, xt, sem)
    n_units = kc_ref.shape[1]
    slab = min(CMP_SLAB, n_units)
    page_rows = pool_ref.shape[2]
    for s in range(n_units // slab):
        base = s * slab * CMP_UNIT
        for p in range(slab * CMP_UNIT // page_rows):
            x = xt[slot, :, base + p * page_rows:base + (p + 1) * page_rows]
            dst = pl.ds(p * page_rows, page_rows)
            kbuf[dst, :] = x[0:KV_HALF, :].T
            vbuf[dst, :] = x[KV_HALF:KV_COLS, :].T
        zka, zkb, zva, zvb = [], [], [], []
        for l in range(CMP_UNIT):
            xk = kbuf[pl.ds(l, slab, stride=CMP_UNIT), :]
            xv = vbuf[pl.ds(l, slab, stride=CMP_UNIT), :]
            zka.append((xk + pea_ref[l:l + 1, 0:KV_HALF]).astype(BF16))
            zkb.append((xk + peb_ref[l:l + 1, 0:KV_HALF]).astype(BF16))
            zva.append((xv + pea_ref[l:l + 1, KV_HALF:KV_COLS]).astype(BF16))
            zvb.append((xv + peb_ref[l:l + 1, KV_HALF:KV_COLS]).astype(BF16))
        rows = pl.ds(s * slab, slab)
        a_k[rows, :] = _dot(jnp.concatenate(zka, axis=1), wka_ref[...])
        b_k[rows, :] = _dot(jnp.concatenate(zkb, axis=1), wkb_ref[...])
        a_v[rows, :] = _dot(jnp.concatenate(zva, axis=1), wva_ref[...])
        b_v[rows, :] = _dot(jnp.concatenate(zvb, axis=1), wvb_ref[...])
    tail = pl.ds(n_units, 8)
    b_k[tail, :] = jnp.zeros((8, b_k.shape[1]), F32)
    b_v[tail, :] = jnp.zeros((8, b_v.shape[1]), F32)
    for s in range(n_units // slab):
        rows = pl.ds(s * slab, slab)
        nxt = pl.ds(s * slab + 1, slab)
        hid_k = jax.nn.gelu(a_k[rows, :] + b_k[nxt, :], approximate=True).astype(BF16)
        hid_v = jax.nn.gelu(a_v[rows, :] + b_v[nxt, :], approximate=True).astype(BF16)
        kc_ref[0, rows, :] = _dot(hid_k, w2k_ref[...]).astype(BF16)
        vct_ref[0, :, rows] = _dot_nt(w2vt_ref[...], hid_v).astype(BF16)


def _compress(page_table, pool, cw):
    b, n_pages = page_table.shape
    page_rows = pool.shape[2]
    rows = n_pages * page_rows
    n_units = rows // CMP_UNIT
    n_slots = min(2, b)
    hid2 = N_KV_A * CMP_HID
    slab_rows = min(CMP_SLAB, n_units) * CMP_UNIT
    assert slab_rows % page_rows == 0 and rows % slab_rows == 0
    const = lambda a: pl.BlockSpec(a.shape, lambda i, pt: (0,) * a.ndim)
    grid_spec = pltpu.PrefetchScalarGridSpec(
        num_scalar_prefetch=1,
        grid=(b,),
        in_specs=[pl.BlockSpec(memory_space=pl.ANY)] + [const(a) for a in cw],
        out_specs=(pl.BlockSpec((1, n_units, KV_HALF), lambda i, pt: (i, 0, 0)),
                   pl.BlockSpec((1, KV_HALF, n_units), lambda i, pt: (i, 0, 0))),
        scratch_shapes=[pltpu.VMEM((n_slots, KV_COLS, rows), F32),
                        pltpu.VMEM((slab_rows, KV_HALF), F32), pltpu.VMEM((slab_rows, KV_HALF), F32),
                        pltpu.VMEM((n_units + 8, hid2), F32), pltpu.VMEM((n_units + 8, hid2), F32),
                        pltpu.VMEM((n_units + 8, hid2), F32), pltpu.VMEM((n_units + 8, hid2), F32),
                        pltpu.SemaphoreType.DMA((n_slots,))],
    )
    return pl.pallas_call(
        _compress_kernel,
        grid_spec=grid_spec,
        out_shape=(jax.ShapeDtypeStruct((b, n_units, KV_HALF), BF16),
                   jax.ShapeDtypeStruct((b, KV_HALF, n_units), BF16)),
        compiler_params=_params("arbitrary"),
        name="compress",
    )(page_table, pool, *cw)


REMOVED = -3.0e38


def _overlap_matrix(n_blocks, n_units):
    cstart = jnp.arange(n_units) * CMP_STRIDE
    bstart = jnp.arange(n_blocks) * SEL_BLOCK
    ov = (cstart[None, :] < bstart[:, None] + SEL_BLOCK) & (cstart[None, :] + CMP_LEN > bstart[:, None])
    return ov.astype(BF16)


def _dot_01(m01, x):
    hi = x.astype(BF16)
    r1 = x - hi.astype(F32)
    mid = r1.astype(BF16)
    lo = (r1 - mid.astype(F32)).astype(BF16)
    return _dot(m01, hi) + _dot(m01, mid) + _dot(m01, lo)


def _softmax_cols(s, mask):
    s = jnp.where(mask, s, NEG)
    mx = jnp.max(s, axis=0, keepdims=True)
    e = jnp.where(mask, jnp.exp2(s - mx), 0.0)
    return e / jnp.maximum(jnp.sum(e, axis=0, keepdims=True), 1e-30)


def _online_init(rows, lanes):
    return (jnp.full((1, lanes), NEG, F32), jnp.zeros((1, lanes), F32), jnp.zeros((rows, lanes), F32))


def _online_update(carry, s, mask, pv):
    m_old, l_old, acc = carry
    s = jnp.where(mask, s, NEG)
    m_new = jnp.maximum(m_old, jnp.max(s, axis=0, keepdims=True))
    alpha = jnp.exp2(m_old - m_new)
    p = jnp.where(mask, jnp.exp2(s - m_new), 0.0)
    l_new = alpha * l_old + jnp.sum(p, axis=0, keepdims=True)
    return m_new, l_new, alpha * acc + pv(p.astype(BF16))


def _online_finish(carry):
    _, l, acc = carry
    return acc / jnp.maximum(l, 1e-30)


def _select_blocks(imp, qpos):
    blk = lax.broadcasted_iota(jnp.int32, imp.shape, 0)
    n_blocks = imp.shape[0]
    cur = qpos // SEL_BLOCK
    forced = (blk == 0) | (blk == cur) | (blk == cur - 1)
    imp = imp + jnp.where(forced, FORCE_BONUS, 0.0)
    work = jnp.where(blk * SEL_BLOCK <= qpos, imp, NEG)
    sel = jnp.zeros(imp.shape, F32)
    for _ in range(N_SEL):
        mx = jnp.max(work, axis=0, keepdims=True)
        first = jnp.min(jnp.where(work == mx, blk, n_blocks), axis=0, keepdims=True)
        hit = blk == first
        sel = jnp.where(hit, 1.0, sel)
        work = jnp.where(hit, REMOVED, work)
    return sel


PROMPT_KV_CHUNK = 1024
PROMPT_KV_PIECE = 128
CHUNK_BLOCKS = PROMPT_KV_CHUNK // SEL_BLOCK
WIN_BLOCKS = WINDOW // Q_BLOCK + 1
SCORE_FLOOR = -1.0e20


def _block_onehot(t):
    blk = (jnp.arange(t) // SEL_BLOCK) % CHUNK_BLOCKS
    return (blk[:, None] == jnp.arange(LANES)[None, :]).astype(BF16)


def _online_update_biased(carry, s_list, pv_list):
    for s, pv in zip(s_list, pv_list):
        m_old, l_old, acc = carry
        m_t, l_t, a_t, p_t = [], [], [], []
        for lt in range(s.shape[1] // LANES):
            cols = slice(lt * LANES, (lt + 1) * LANES)
            st = s[:, cols]
            m_new = jnp.maximum(m_old[:, cols], jnp.max(st, axis=0, keepdims=True))
            alpha = jnp.exp2(m_old[:, cols] - m_new)
            p = jnp.exp2(st - m_new)
            m_t.append(m_new)
            a_t.append(alpha)
            l_t.append(alpha * l_old[:, cols] + jnp.sum(p, axis=0, keepdims=True))
            p_t.append(p.astype(BF16))
        cat = lambda parts: parts[0] if len(parts) == 1 else jnp.concatenate(parts, axis=1)
        carry = (cat(m_t), cat(l_t), cat(a_t) * acc + pv(cat(p_t)))
    return carry


def _nsa_prompt_kernel(qt_ref, kc_ref, vct_ref, ovl_ref, ks_ref, vst_ref, *rest):
    kw_refs = rest[0:WIN_BLOCKS]
    vwt_refs = rest[WIN_BLOCKS:2 * WIN_BLOCKS]
    gt_ref, o_ref, sel_ref = rest[2 * WIN_BLOCKS:]
    i = pl.program_id(0)
    start = i * Q_BLOCK
    lanes = GROUP_R * Q_BLOCK
    qpos1 = start + lax.broadcasted_iota(jnp.int32, (1, Q_BLOCK), 1)
    qpos = jnp.concatenate([qpos1] * GROUP_R, axis=1)
    n_units = kc_ref.shape[0]
    ck = PROMPT_KV_CHUNK
    piece = PROMPT_KV_PIECE
    groups = range(N_KV_A)
    rows_of = [slice(g * HEAD_DIM, (g + 1) * HEAD_DIM) for g in groups]
    qgs, o_cs, o_ws = [], [], []
    for g in groups:
        top = jnp.concatenate([qt_ref[(g * GROUP_R + r) * HEAD_DIM:(g * GROUP_R + r + 1) * HEAD_DIM, :]
                               for r in range(GROUP_R)], axis=1)
        zero = jnp.zeros_like(top)
        qg = jnp.concatenate([top, zero] if g == 0 else [zero, top], axis=0)
        rows_g = rows_of[g]
        qgs.append(qg)
        unit = lax.broadcasted_iota(jnp.int32, (n_units, lanes), 0)
        p_c = _softmax_cols(_dot(kc_ref[...], qg), unit * CMP_STRIDE + (CMP_LEN - 1) <= qpos)
        o_c = _dot(vct_ref[rows_g, :], p_c.astype(BF16))
        p_sum = p_c[:, 0:Q_BLOCK]
        for r in range(1, GROUP_R):
            p_sum = p_sum + p_c[:, r * Q_BLOCK:(r + 1) * Q_BLOCK]
        sel_ref[g] = (1.0 - _select_blocks(_dot_01(ovl_ref[...], p_sum), qpos1)) * NEG
        o_cs.append(o_c)
        carry = _online_init(HEAD_DIM, lanes)
        for j in range(WIN_BLOCKS):
            wpos = start - WINDOW + j * Q_BLOCK + lax.broadcasted_iota(jnp.int32, (Q_BLOCK, lanes), 0)
            dpos = qpos - wpos
            mask = (dpos >= 0) & (dpos < WINDOW) & (wpos >= 0)
            vwt = vwt_refs[j]
            carry = _online_update(carry, _dot(kw_refs[j][...], qg), mask,
                                   lambda p, vwt=vwt, rows_g=rows_g: _dot(vwt[rows_g, :], p))
        o_ws.append(_online_finish(carry))

    def sel_scores(g, c):
        bias = sel_ref[g, pl.ds(pl.multiple_of(c * CHUNK_BLOCKS, CHUNK_BLOCKS), CHUNK_BLOCKS), :]
        bias = jnp.concatenate([bias] * GROUP_R, axis=1).astype(BF16)
        w = jnp.concatenate([qgs[g], bias, jnp.zeros((KV_HALF - CHUNK_BLOCKS, lanes), BF16)], axis=0)
        k0 = pl.multiple_of(c * ck, ck)
        starts = [k0 + h * piece for h in range(ck // piece)]
        s_list = [_dot(ks_ref[pl.ds(k, piece), :], w) for k in starts]
        pv_list = [lambda p, k=k: _dot(vst_ref[rows_of[g], pl.ds(k, piece)], p) for k in starts]
        return starts, s_list, pv_list

    def sel_update(c, carries, causal):
        parts = [sel_scores(g, c) for g in groups]
        carries = list(carries)
        row = lax.broadcasted_iota(jnp.int32, (piece, lanes), 0)
        for h in range(ck // piece):
            for g in groups:
                starts, s_list, pv_list = parts[g]
                s = jnp.where(starts[h] + row <= qpos, s_list[h], NEG) if causal else s_list[h]
                carries[g] = _online_update_biased(carries[g], [s], [pv_list[h]])
        return tuple(carries)

    n_full = start // ck
    init = (jnp.full((1, lanes), SCORE_FLOOR, F32),) + _online_init(HEAD_DIM, lanes)[1:]
    carries = lax.fori_loop(0, n_full, lambda c, cs: sel_update(c, cs, False), (init,) * N_KV_A)
    carries = sel_update(n_full, carries, True)
    for g in groups:
        o_c, o_w, o_s = o_cs[g], o_ws[g], _online_finish(carries[g])
        for r in range(GROUP_R):
            h = g * GROUP_R + r
            ln = slice(r * Q_BLOCK, (r + 1) * Q_BLOCK)
            o = (gt_ref[h:h + 1, :] * o_c[:, ln] + gt_ref[N_HEADS_A + h:N_HEADS_A + h + 1, :] * o_s[:, ln]
                 + gt_ref[2 * N_HEADS_A + h:2 * N_HEADS_A + h + 1, :] * o_w[:, ln])
            o_ref[h * HEAD_DIM:(h + 1) * HEAD_DIM, :] = o.astype(o_ref.dtype)


def _nsa_prompt(qt, kc, vct, ks, vst, kw, vwt, small_t):
    t = qt.shape[1]
    n_units = kc.shape[0]
    n_blocks = t // SEL_BLOCK
    ovl = _overlap_matrix(n_blocks, n_units)
    assert t % PROMPT_KV_CHUNK == 0
    ks = jnp.concatenate([ks, _block_onehot(t)], axis=1)
    nq = t // Q_BLOCK
    const = lambda a: pl.BlockSpec(a.shape, lambda i: (0, 0))
    back = WINDOW // Q_BLOCK
    kw_specs = [pl.BlockSpec((Q_BLOCK, KV_HALF), lambda i, j=j: (jnp.maximum(i - back + j, 0), 0))
                for j in range(WIN_BLOCKS)]
    vwt_specs = [pl.BlockSpec((KV_HALF, Q_BLOCK), lambda i, j=j: (0, jnp.maximum(i - back + j, 0)))
                 for j in range(WIN_BLOCKS)]
    return pl.pallas_call(
        _nsa_prompt_kernel,
        grid=(nq,),
        in_specs=[pl.BlockSpec((A_WIDTH, Q_BLOCK), lambda i: (0, i)), const(kc), const(vct), const(ovl),
                  const(ks), const(vst)] + kw_specs + vwt_specs
                 + [pl.BlockSpec((SMALL_COLS, Q_BLOCK), lambda i: (0, i))],
        out_specs=pl.BlockSpec((A_WIDTH, Q_BLOCK), lambda i: (0, i)),
        out_shape=jax.ShapeDtypeStruct((A_WIDTH, t), BF16),
        scratch_shapes=[pltpu.VMEM((N_KV_A, n_blocks, Q_BLOCK), F32)],
        compiler_params=_params("arbitrary"),
        name="nsa_prompt",
    )(qt, kc, vct, ovl, ks, vst, *([kw] * WIN_BLOCKS), *([vwt] * WIN_BLOCKS), small_t)


NEW_ROWS_PAD = 16


def _split3(x):
    hi = x.astype(BF16)
    r1 = x - hi.astype(F32)
    mid = r1.astype(BF16)
    return hi, mid, (r1 - mid.astype(F32)).astype(BF16)


def _nsa_sample_kernel(pt_ref, pool_ref, qbd_ref, kc_ref, vct_ref, ovl_ref, rep_ref, oh_ref, new_ref, wint_ref,
                       wnew_ref, gl_ref, o_ref, xt, sel_ref, sem, *, t_new):
    slot = _gather_pipeline(pt_ref, pool_ref, xt, sem)
    past = xt.shape[2]
    lanes = LANES
    qbd = qbd_ref[0]
    qpos = past + lax.broadcasted_iota(jnp.int32, (1, lanes), 1) % t_new
    n_units = kc_ref.shape[1]
    unit = lax.broadcasted_iota(jnp.int32, (n_units, lanes), 0)
    p_c = _softmax_cols(_dot(kc_ref[0], qbd), unit * CMP_STRIDE + (CMP_LEN - 1) <= qpos)
    o_c = _dot(vct_ref[0], p_c.astype(BF16))
    imp = _dot_01(ovl_ref[...], p_c)
    imp = sum(_dot(t, rep_ref[...]) for t in _split3(imp))
    sel_ref[...] = (1.0 - _select_blocks(imp, qpos)) * NEG
    ck = PROMPT_KV_CHUNK
    piece = PROMPT_KV_PIECE

    def sel_step(c, carry):
        bias = sel_ref[c * CHUNK_BLOCKS:(c + 1) * CHUNK_BLOCKS, :].astype(BF16)
        w = jnp.concatenate([qbd, bias, jnp.zeros((KV_HALF - CHUNK_BLOCKS, lanes), BF16)], axis=0)
        k0 = c * ck
        s_list, pv_list = [], []
        for h in range(ck // piece):
            k = k0 + h * piece
            kt = xt[slot, 0:KV_HALF, pl.ds(k, piece)].astype(BF16)
            s_list.append(_dot_tn(jnp.concatenate([kt, oh_ref[:, h * piece:(h + 1) * piece]], axis=0), w))
            pv_list.append(lambda p, k=k: _dot(xt[slot, KV_HALF:KV_COLS, pl.ds(k, piece)].astype(BF16), p))
        return _online_update_biased(carry, s_list, pv_list)

    carry = (jnp.full((1, lanes), SCORE_FLOOR, F32),) + _online_init(KV_HALF, lanes)[1:]
    for c in range(past // ck):
        carry = sel_step(c, carry)

    def new_rows(ref):
        x = ref[0]
        pad = jnp.zeros((NEW_ROWS_PAD - t_new, KV_COLS), F32)
        x = jnp.concatenate([x, pad], axis=0).astype(BF16)
        return x[:, 0:KV_HALF], x[:, KV_HALF:KV_COLS]

    row = lax.broadcasted_iota(jnp.int32, (NEW_ROWS_PAD, lanes), 0)
    new_ok = (row < t_new) & (past + row <= qpos)
    k_new, v_new = new_rows(new_ref)
    s_new = _dot(k_new, qbd) + sel_ref[pl.ds(past // SEL_BLOCK, 1), :]
    carry = _online_update_biased(carry, [jnp.where(new_ok, s_new, NEG)], [lambda p: _dot_tn(v_new, p)])
    o_s = _online_finish(carry)
    n_win = wint_ref.shape[2]
    wpos = past - n_win + lax.broadcasted_iota(jnp.int32, (n_win, lanes), 0)
    dpos = qpos - wpos
    w_mask = (dpos >= 0) & (dpos < WINDOW) & (wpos >= 0)
    s_w = _dot_tn(wint_ref[0, 0:KV_HALF, :].astype(BF16), qbd)
    carry = _online_update(_online_init(KV_HALF, lanes), s_w, w_mask,
                           lambda p: _dot(wint_ref[0, KV_HALF:KV_COLS, :].astype(BF16), p))
    k_wn, v_wn = new_rows(wnew_ref)
    carry = _online_update(carry, _dot(k_wn, qbd), new_ok & (qpos - past - row < WINDOW), lambda p: _dot_tn(v_wn, p))
    o_w = _online_finish(carry)
    o_ref[0] = gl_ref[0, 0:1, :] * o_c + gl_ref[0, 1:2, :] * o_s + gl_ref[0, 2:3, :] * o_w


def _nsa_sample(page_table, pool, q, kc, vct, kvs_new, win_state, kvw_new, gates):
    b, n_pages = page_table.shape
    t_new = q.shape[1]
    page_rows = pool.shape[2]
    past = n_pages * page_rows
    n_units = kc.shape[1]
    assert t_new <= SEL_BLOCK and t_new <= NEW_ROWS_PAD and past % PROMPT_KV_CHUNK == 0
    oh = _block_onehot(PROMPT_KV_CHUNK).T
    used = N_KV_A * GROUP_R * t_new
    assert used <= LANES
    qs = q.reshape(b, t_new, N_KV_A, GROUP_R, HEAD_DIM).transpose(0, 2, 4, 3, 1)
    qs = qs.reshape(b, N_KV_A, HEAD_DIM, GROUP_R * t_new)
    qbd = jnp.zeros((b, N_KV_A, HEAD_DIM, N_KV_A, GROUP_R * t_new), BF16)
    for g in range(N_KV_A):
        qbd = qbd.at[:, g, :, g, :].set(qs[:, g])
    qbd = jnp.pad(qbd.reshape(b, KV_HALF, used), ((0, 0), (0, 0), (0, LANES - used)))
    lane = jnp.arange(LANES)
    same = ((lane[:, None] // (GROUP_R * t_new) == lane[None, :] // (GROUP_R * t_new))
            & (lane[:, None] % t_new == lane[None, :] % t_new) & (lane[:, None] < used) & (lane[None, :] < used))
    rep = same.astype(BF16)
    gl = gates.reshape(b, t_new, 3, N_KV_A, GROUP_R).transpose(0, 2, 3, 4, 1).reshape(b, 3, used)
    gl = jnp.pad(gl, ((0, 0), (0, 8 - 3), (0, LANES - used)))
    n_blocks = -(-(past + t_new) // SEL_BLOCK)
    n_blocks_pad = -(-n_blocks // 8) * 8
    ovl = _overlap_matrix(n_blocks_pad, n_units)
    n_slots = min(2, b)
    const = lambda a: pl.BlockSpec(a.shape, lambda i, pt: (0,) * a.ndim)
    per_seq = lambda a: pl.BlockSpec((1,) + a.shape[1:], lambda i, pt: (i,) + (0,) * (a.ndim - 1))
    kvs_new = kvs_new.reshape(b, t_new, KV_COLS)
    kvw_new = kvw_new.reshape(b, t_new, KV_COLS)
    grid_spec = pltpu.PrefetchScalarGridSpec(
        num_scalar_prefetch=1,
        grid=(b,),
        in_specs=[pl.BlockSpec(memory_space=pl.ANY), per_seq(qbd), per_seq(kc), per_seq(vct), const(ovl), const(rep),
                  const(oh), per_seq(kvs_new), per_seq(win_state), per_seq(kvw_new), per_seq(gl)],
        out_specs=pl.BlockSpec((1, KV_HALF, LANES), lambda i, pt: (i, 0, 0)),
        scratch_shapes=[pltpu.VMEM((n_slots, KV_COLS, past), F32), pltpu.VMEM((n_blocks_pad, LANES), F32),
                        pltpu.SemaphoreType.DMA((n_slots,))],
    )
    o = pl.pallas_call(
        functools.partial(_nsa_sample_kernel, t_new=t_new),
        grid_spec=grid_spec,
        out_shape=jax.ShapeDtypeStruct((b, KV_HALF, LANES), F32),
        compiler_params=_params("arbitrary"),
        name="nsa_sample",
    )(page_table, pool, qbd, kc, vct, ovl, rep, oh, kvs_new, win_state, kvw_new, gl)
    o = o[:, :, :used].reshape(b, N_KV_A, HEAD_DIM, N_KV_A, GROUP_R, t_new)
    o = jnp.stack([o[:, g, :, g] for g in range(N_KV_A)], axis=1)
    return o.transpose(0, 4, 1, 3, 2).reshape(b * t_new, A_WIDTH)


CONV_PAD = 8


def _log_sigmoid(x):
    return jnp.minimum(x, 0.0) - jnp.log(1.0 + jnp.exp(-jnp.abs(x)))


def _mlstm_kernel(u_ref, vm_ref, om_ref, sm_ref, ift_ref, cw_ref, cb_ref, wq_ref, wk_ref, mg_ref,
                  c0_ref, n0_ref, m0_ref, conv0_ref, hm_ref, c_out, n_out, m_out, conv_out,
                  c_s, n_s, m_s, ext_s):
    c = pl.program_id(1)
    chunk = u_ref.shape[0]
    keep = CONV_W - 1

    @pl.when(c == 0)
    def _():
        c_s[...] = c0_ref[0]
        n_s[...] = n0_ref[0]
        m_s[...] = m0_ref[0]
        ext_s[0:CONV_PAD, :] = conv0_ref[0]

    ext_s[CONV_PAD:CONV_PAD + chunk, :] = u_ref[...]
    conv = cb_ref[...]
    for j in range(CONV_W):
        conv = conv + ext_s[pl.ds(CONV_PAD - keep + j, chunk), :] * cw_ref[j:j + 1, :]
    ext_s[CONV_PAD - keep:CONV_PAD, :] = ext_s[CONV_PAD + chunk - keep:CONV_PAD + chunk, :]
    xc = (conv * jax.nn.sigmoid(conv)).astype(BF16)
    ti = lax.broadcasted_iota(jnp.int32, (chunk, chunk), 0)
    si = lax.broadcasted_iota(jnp.int32, (chunk, chunk), 1)
    causal = si <= ti
    head_lane = lax.broadcasted_iota(jnp.int32, (1, LANES), 1)
    sm = sm_ref[...]
    m_all = m_s[...]
    for h in range(N_HEADS_M):
        cols = slice(h * HEAD_DIM_M, (h + 1) * HEAD_DIM_M)
        xh = xc[:, cols]
        qm = _dot(xh, wq_ref[h])
        km = _dot(xh, wk_ref[h]) * HEAD_DIM_M ** -0.5
        qb, kb = qm.astype(BF16), km.astype(BF16)
        vb = vm_ref[:, cols].astype(BF16)
        ig_col = sm[:, SM_IG + h:SM_IG + h + 1]
        lf_col = _log_sigmoid(sm[:, SM_FG + h:SM_FG + h + 1])
        ig_row = ift_ref[0, 0, h:h + 1, :]
        lf_row = _log_sigmoid(ift_ref[0, 0, N_HEADS_M + h:N_HEADS_M + h + 1, :])
        fcum_col = jnp.sum(jnp.where(causal, lf_row, 0.0), axis=1, keepdims=True)
        fcum_row = jnp.sum(jnp.where(ti <= si, lf_col, 0.0), axis=0, keepdims=True)
        d_row = ig_row - fcum_row
        cmax_col = jnp.max(jnp.where(causal, d_row, NEG), axis=1, keepdims=True)
        m_prev = m_all[:, h:h + 1]
        m_row = fcum_col + jnp.maximum(m_prev, cmax_col)
        w_intra = jnp.exp(jnp.where(causal, (fcum_col - m_row) + d_row, NEG))
        w_inter = jnp.exp(m_prev + fcum_col - m_row)
        c_prev = c_s[h]
        n_prev = n_s[h:h + 1, :]
        qk = _dot_nt(qb, kb) * w_intra
        num = w_inter * _dot(qb, c_prev.astype(BF16)) + _dot(qk.astype(BF16), vb)
        den = w_inter * jnp.sum(qm * n_prev, axis=1, keepdims=True) + jnp.sum(qk, axis=1, keepdims=True)
        hh = num / jnp.maximum(jnp.abs(den), jnp.exp(-m_row))
        m_new = m_row[chunk - 1:chunk, :]
        f_last = fcum_col[chunk - 1:chunk, :]
        w_keep = jnp.exp(m_prev + f_last - m_new)
        w_src = jnp.exp(ig_col + f_last - fcum_col - m_new)
        ks = km * w_src
        c_s[h] = w_keep * c_prev + _dot_tn(ks.astype(BF16), vb)
        n_s[h:h + 1, :] = w_keep * n_prev + jnp.sum(ks, axis=0, keepdims=True)
        m_all = jnp.where(head_lane == h, m_new, m_all)
        y = hh * lax.rsqrt(jnp.mean(hh * hh, axis=1, keepdims=True) + EPS) * mg_ref[:, cols]
        hm_ref[:, cols] = y * jax.nn.sigmoid(om_ref[:, cols])
    m_s[...] = m_all

    @pl.when(c == pl.num_programs(1) - 1)
    def _():
        c_out[0] = c_s[...]
        n_out[0] = n_s[...]
        m_out[0] = m_s[...]
        conv_out[0] = ext_s[0:CONV_PAD, :]


def _mlstm(u, vm, om, small, conv_w, conv_b, w_mq, w_mk, mnorm_g, c0, n0, m0, conv0, b, t, chunk):
    n_chunks = t // chunk
    ift = small[:, SM_IG:SM_IG + 2 * N_HEADS_M].reshape(b, n_chunks, chunk, 2 * N_HEADS_M).transpose(0, 1, 3, 2)
    m0p = jnp.pad(m0.reshape(b, 1, N_HEADS_M), ((0, 0), (0, 0), (0, LANES - N_HEADS_M)))
    conv0p = jnp.pad(conv0, ((0, 0), (CONV_PAD - (CONV_W - 1), 0), (0, 0)))
    row = lambda w: pl.BlockSpec((chunk, w), lambda i, c: (i * n_chunks + c, 0))
    const = lambda a: pl.BlockSpec(a.shape, lambda i, c: (0,) * a.ndim)
    per_seq = lambda a: pl.BlockSpec((1,) + a.shape[1:], lambda i, c: (i,) + (0,) * (a.ndim - 1))
    wq = w_mq.astype(BF16)
    wk = w_mk.astype(BF16)
    cb = conv_b.reshape(1, M_WIDTH)
    mg = mnorm_g.reshape(1, M_WIDTH)
    out_shape = (jax.ShapeDtypeStruct((b * t, M_WIDTH), F32),
                 jax.ShapeDtypeStruct(c0.shape, F32), jax.ShapeDtypeStruct(n0.shape, F32),
                 jax.ShapeDtypeStruct(m0p.shape, F32), jax.ShapeDtypeStruct(conv0p.shape, F32))
    hm, c_new, n_new, m_new, conv_new = pl.pallas_call(
        _mlstm_kernel,
        grid=(b, n_chunks),
        in_specs=[row(M_WIDTH), row(M_WIDTH), row(M_WIDTH), row(SMALL_COLS),
                  pl.BlockSpec((1, 1, 2 * N_HEADS_M, chunk), lambda i, c: (i, c, 0, 0)),
                  const(conv_w), const(cb), const(wq), const(wk), const(mg),
                  per_seq(c0), per_seq(n0), per_seq(m0p), per_seq(conv0p)],
        out_specs=(row(M_WIDTH), per_seq(c0), per_seq(n0), per_seq(m0p), per_seq(conv0p)),
        out_shape=out_shape,
        scratch_shapes=[pltpu.VMEM(c0.shape[1:], F32), pltpu.VMEM(n0.shape[1:], F32), pltpu.VMEM((1, LANES), F32),
                        pltpu.VMEM((CONV_PAD + chunk, M_WIDTH), F32)],
        compiler_params=_params("arbitrary", "arbitrary"),
        name="mlstm",
    )(u, vm, om, small, ift, conv_w, cb, wq, wk, mg, c0, n0, m0p, conv0p)
    return hm, c_new, n_new, m_new[:, 0, :N_HEADS_M], conv_new[:, CONV_PAD - (CONV_W - 1):, :]


def _outproj_kernel(x_ref, oa_ref, hm_ref, wo_ref, g1_ref, ng_ref, sc_ref, sh_ref, wq_ref,
                    x1_ref, h2_ref, pq_ref, *, oa_transposed):
    oa = oa_ref[...].astype(BF16)
    wo_a = wo_ref[0:A_WIDTH, :]
    mix = _dot_tn(oa, wo_a) if oa_transposed else _dot(oa, wo_a)
    mix = mix + _dot(hm_ref[...].astype(BF16), wo_ref[A_WIDTH:A_WIDTH + M_WIDTH, :])
    x1 = x_ref[...] + g1_ref[...] * mix
    x1_ref[...] = x1
    y = x1 * lax.rsqrt(jnp.mean(x1 * x1, axis=-1, keepdims=True) + EPS) * ng_ref[...]
    h2 = (y * (1.0 + sc_ref[...]) + sh_ref[...]).astype(BF16)
    h2_ref[...] = h2
    pq_ref[...] = _dot(h2, wq_ref[...]).astype(BF16)


def _outproj(x, oa, hm, w_out, g1, norm_g, sc, sh, peer_wq, tm, oa_transposed):
    r, d = x.shape
    per_row = g1.shape[0] != 1
    mod = pl.BlockSpec((tm, d), lambda i: (i, 0)) if per_row else pl.BlockSpec((1, d), lambda i: (0, 0))
    row = lambda w: pl.BlockSpec((tm, w), lambda i: (i, 0))
    const = lambda a: pl.BlockSpec(a.shape, lambda i: (0, 0))
    oa_spec = pl.BlockSpec((A_WIDTH, tm), lambda i: (0, i)) if oa_transposed else row(A_WIDTH)
    nq = peer_wq.shape[1]
    return pl.pallas_call(
        functools.partial(_outproj_kernel, oa_transposed=oa_transposed),
        grid=(r // tm,),
        in_specs=[row(d), oa_spec, row(M_WIDTH), const(w_out), mod, const(norm_g), mod, mod, const(peer_wq)],
        out_specs=(row(d), row(d), row(nq)),
        out_shape=(jax.ShapeDtypeStruct((r, d), F32), jax.ShapeDtypeStruct((r, d), BF16),
                   jax.ShapeDtypeStruct((r, nq), BF16)),
        compiler_params=_params("arbitrary"),
        name="outproj",
    )(x, oa, hm, w_out, g1, norm_g, sc, sh, peer_wq)


PEER_CHUNK_I = 8
PEER_HALF = PEER_DKEY // 2


SUBLANES = 8


def _oddeven_merge(lo, hi, r):
    step = r * 2
    if step < hi - lo:
        yield from _oddeven_merge(lo, hi, step)
        yield from _oddeven_merge(lo + r, hi, step)
        yield from [(i, i + r) for i in range(lo + r, hi - r, step)]
    else:
        yield (lo, lo + r)


def _oddeven_sort(lo, hi):
    if hi - lo >= 1:
        mid = lo + (hi - lo) // 2
        yield from _oddeven_sort(lo, mid)
        yield from _oddeven_sort(mid + 1, hi)
        yield from _oddeven_merge(lo, hi, 1)


def _top_values(x):
    k = PEER_TOPK
    r = [x[v * SUBLANES:(v + 1) * SUBLANES] for v in range(k)]

    def exchange(t, i, j):
        t[i], t[j] = jnp.maximum(t[i], t[j]), jnp.minimum(t[i], t[j])

    for i, j in _oddeven_sort(0, k - 1):
        exchange(r, i, j)
    shift = SUBLANES // 2
    while shift >= 1:
        r = [jnp.maximum(r[v], pltpu.roll(r[k - 1 - v], shift, 0)) for v in range(k)]
        d = k // 2
        while d >= 1:
            for v in range(k):
                if v & d == 0:
                    exchange(r, v, v + d)
            d //= 2
        shift //= 2
    return jnp.concatenate([t[0:1] for t in r], axis=0)


def _peer_kernel(pq_ref, keys_ref, h2t_ref, u_ref, vt_ref, x1_ref, g2_ref, ng_ref, y_ref,
                 s_s, top_s, thr_s, e1_s, e2_s, act_s, w_s, acc_s):
    j = pl.program_id(1)
    tt = pq_ref.shape[0]
    k = PEER_TOPK

    @pl.when(j == 0)
    def _():
        def score(hc, _):
            col = pl.multiple_of(hc * PEER_HALF, PEER_HALF)
            s = _dot_nt(keys_ref[hc], pq_ref[:, pl.ds(col, PEER_HALF)])
            s_s[hc] = s
            top_s[hc] = _top_values(s)
            return 0
        lax.fori_loop(0, 2 * PEER_HEADS, score, 0)

        def candidates(x, y):
            pieces = [x[0:1] * y]
            pieces += [x[i:i + 1] * y[0:k // 2] for i in range(1, k // 2)]
            pieces += [x[k // 2:k] * y[0:1]]
            n_rows = sum(p.shape[0] for p in pieces)
            pieces += [jnp.full((k * SUBLANES - n_rows, tt), -1.0, F32)]
            return jnp.concatenate(pieces, axis=0)

        def head(h, _):
            a = top_s[2 * h]
            b = top_s[2 * h + 1]
            ea = jnp.exp(a - a[0:1])
            eb = jnp.exp(b - b[0:1])
            z = jnp.sum(_top_values(candidates(ea, eb)), axis=0, keepdims=True)
            thr = _top_values(candidates(ea, eb / z))[k - 1:k]
            thr_s[h] = jnp.broadcast_to(thr, thr_s.shape[1:])
            e1_s[h] = jnp.exp(s_s[2 * h] - a[0:1])
            e2_s[h] = jnp.exp(s_s[2 * h + 1] - b[0:1]) / z
            return 0
        lax.fori_loop(0, PEER_HEADS, head, 0)
        acc_s[...] = jnp.zeros(acc_s.shape, F32)

    act_s[...] = jax.nn.gelu(_dot(u_ref[...], h2t_ref[...]), approximate=True)

    def build(ii, _):
        i = j * PEER_CHUNK_I + ii
        r = pl.ds(pl.multiple_of(ii * PEER_NKEYS, PEER_NKEYS), PEER_NKEYS)
        e1_rows = [e1_s[h, pl.ds(i, 1), :] for h in range(PEER_HEADS)]
        for lt in range(tt // LANES):
            cols = slice(lt * LANES, (lt + 1) * LANES)
            g = jnp.zeros((PEER_NKEYS, LANES), F32)
            for h in range(PEER_HEADS):
                val = e2_s[h, :, cols] * e1_rows[h][:, cols]
                g = g + jnp.where(val >= thr_s[h, 0:1, cols], val, 0.0)
            w_s[r, cols] = (g * act_s[r, cols]).astype(BF16)
        return 0
    lax.fori_loop(0, PEER_CHUNK_I, build, 0)
    acc_s[...] += _dot(vt_ref[...], w_s[...])

    @pl.when(j == pl.num_programs(1) - 1)
    def _():
        x2 = x1_ref[...] + g2_ref[...] * acc_s[...].T
        y_ref[...] = x2 * lax.rsqrt(jnp.mean(x2 * x2, axis=-1, keepdims=True) + EPS) * ng_ref[...]


def _peer(pq, keys, h2t, u_tab, vt_tab, x1, g2, norm_g, tt):
    n_tok, d = x1.shape
    n_exp = u_tab.shape[0]
    ec = PEER_CHUNK_I * PEER_NKEYS
    per_row = g2.shape[0] != 1
    mod = pl.BlockSpec((tt, d), lambda t, j: (t, 0)) if per_row else pl.BlockSpec((1, d), lambda t, j: (0, 0))
    f32_tok = lambda n: pltpu.VMEM((n, PEER_NKEYS, tt), F32)
    return pl.pallas_call(
        _peer_kernel,
        grid=(n_tok // tt, n_exp // ec),
        in_specs=[pl.BlockSpec((tt, pq.shape[1]), lambda t, j: (t, 0)),
                  pl.BlockSpec(keys.shape, lambda t, j: (0, 0, 0)),
                  pl.BlockSpec((d, tt), lambda t, j: (0, t)),
                  pl.BlockSpec((ec, d), lambda t, j: (j, 0)),
                  pl.BlockSpec((d, ec), lambda t, j: (0, j)),
                  pl.BlockSpec((tt, d), lambda t, j: (t, 0)), mod,
                  pl.BlockSpec((1, d), lambda t, j: (0, 0))],
        out_specs=pl.BlockSpec((tt, d), lambda t, j: (t, 0)),
        out_shape=jax.ShapeDtypeStruct((n_tok, d), F32),
        scratch_shapes=[f32_tok(2 * PEER_HEADS), pltpu.VMEM((2 * PEER_HEADS, PEER_TOPK, tt), F32),
                        pltpu.VMEM((PEER_HEADS, 8, tt), F32), f32_tok(PEER_HEADS), f32_tok(PEER_HEADS),
                        pltpu.VMEM((ec, tt), F32), pltpu.VMEM((ec, tt), BF16), pltpu.VMEM((d, tt), F32)],
        compiler_params=_params("arbitrary", "arbitrary"),
        name="peer",
    )(pq, keys, h2t, u_tab, vt_tab, x1, g2, norm_g)


def kernel(x_prompt, x_sample, cache_cmp_kv, cache_slc_kv, page_table, state_win_kv, state_C, state_n, state_m, state_conv, c_prompt, c_sample, w_ada, b_ada, norm1_g, w_in, b_gate_a, cmp_pe_k, cmp_w1_k, cmp_w2_k, cmp_pe_v, cmp_w1_v, cmp_w2_v, conv_w, conv_b, w_mq, w_mk, b_i, b_f, mnorm_g, w_out, norm2_g, peer_wq, peer_keys, peer_u, peer_v, normf_g):
    b_p, t_p, dm = x_prompt.shape
    b_s, t_s, _ = x_sample.shape
    page_rows = cache_cmp_kv.shape[1]
    past_len = page_table.shape[1] * page_rows
    kv_shape = (2, N_KV_A, HEAD_DIM)

    c_all = jnp.concatenate([c_prompt, c_sample], axis=0)
    ada = _ada(jnp.pad(c_all, ((0, -c_all.shape[0] % 8), (0, 0))), w_ada, b_ada)
    w_in_p, bias_small = _permute_w_in(w_in, b_gate_a, b_i, b_f)
    cw = _compress_weights(cmp_pe_k, cmp_w1_k, cmp_w2_k, cmp_pe_v, cmp_w1_v, cmp_w2_v)
    w_out_b = w_out.astype(BF16)
    wq_b = peer_wq.astype(BF16)
    keys_b = peer_keys.reshape(2 * PEER_HEADS, PEER_NKEYS, PEER_HALF).astype(BF16)
    u_b = peer_u.astype(BF16)
    vt_b = peer_v.T.astype(BF16)
    row_vec = lambda v: v.reshape(1, -1)

    def modulation(rows, t):
        parts = [rows[:, k * dm:(k + 1) * dm] for k in range(6)]
        return parts if rows.shape[0] == 1 else [jnp.repeat(p, t, axis=0) for p in parts]

    def tile_rows(n):
        return min(256, n)

    def peer_stage(pq, h2, x1, g2):
        n = x1.shape[0]
        tt = min(512, -(-n // LANES) * LANES)
        pad = -n % tt
        padr = lambda a: jnp.pad(a, ((0, pad), (0, 0)))
        g2p = g2 if g2.shape[0] == 1 else padr(g2)
        y = _peer(padr(pq), keys_b, padr(h2).T, u_b, vt_b, padr(x1), g2p, row_vec(normf_g), tt)
        return y[:n]

    sh1, sc1, g1, sh2, sc2, g2 = modulation(ada[0:b_p], t_p)
    xp = x_prompt.reshape(b_p * t_p, dm)
    pos_p = jnp.tile(jnp.arange(t_p), b_p)
    q, kvc_p, kvs_p, kvw_p, u, vm, om, sm = _inproj(xp, row_vec(norm1_g), sc1, sh1, w_in_p, bias_small,
                                                    _rot_tables(pos_p), tile_rows(b_p * t_p))
    table = jnp.arange(t_p // page_rows, dtype=jnp.int32).reshape(1, -1)
    sm_t = sm.T
    oa_t = []
    for s in range(b_p):
        rows = slice(s * t_p, (s + 1) * t_p)
        pool_p = kvc_p[rows].reshape(t_p // page_rows, page_rows, KV_COLS).transpose(0, 2, 1)
        kc, vct = _compress(table, pool_p, cw)
        ks, kw = kvs_p[rows], kvw_p[rows]
        oa_t.append(_nsa_prompt(q[rows].T, kc[0], vct[0], ks[:, :KV_HALF].astype(BF16),
                                ks[:, KV_HALF:].astype(BF16).T, kw[:, :KV_HALF].astype(BF16),
                                kw[:, KV_HALF:].astype(BF16).T, sm_t[:, rows]))
    oa_t = oa_t[0] if b_p == 1 else jnp.concatenate(oa_t, axis=1)
    hm, c_p, n_p, m_p, conv_p = _mlstm(
        u, vm, om, sm, conv_w, conv_b, w_mq, w_mk, mnorm_g,
        jnp.zeros((b_p, N_HEADS_M, HEAD_DIM_M, HEAD_DIM_M), F32), jnp.zeros((b_p, N_HEADS_M, HEAD_DIM_M), F32),
        jnp.zeros((b_p, N_HEADS_M), F32), jnp.zeros((b_p, CONV_W - 1, M_WIDTH), F32), b_p, t_p, min(128, t_p))
    x1, h2, pq = _outproj(xp, oa_t, hm, w_out_b, g1, row_vec(norm2_g), sc2, sh2, wq_b, tile_rows(b_p * t_p), True)
    y_p = peer_stage(pq, h2, x1, g2).reshape(b_p, t_p, dm)

    sh1, sc1, g1, sh2, sc2, g2 = modulation(ada[b_p:b_p + b_s], t_s)
    xs = x_sample.reshape(b_s * t_s, dm)
    pos_s = jnp.tile(past_len + jnp.arange(t_s), b_s)
    q, kvc_s, kvs_s, kvw_s, u, vm, om, sm = _inproj(xs, row_vec(norm1_g), sc1, sh1, w_in_p, bias_small,
                                                    _rot_tables(pos_s), tile_rows(b_s * t_s))
    kc, vct = _compress(page_table, _transposed_pool(cache_cmp_kv), cw)
    n_win = state_win_kv.shape[1]
    oa = _nsa_sample(page_table, _transposed_pool(cache_slc_kv), q.reshape(b_s, t_s, A_WIDTH),
                     kc, vct, kvs_s, _transposed_pool(state_win_kv), kvw_s,
                     sm[:, 0:GATE_COLS].reshape(b_s, t_s, GATE_COLS))
    hm, c_s, n_s, m_s, conv_s = _mlstm(u, vm, om, sm, conv_w, conv_b, w_mq, w_mk, mnorm_g, state_C.astype(F32),
                                       state_n.astype(F32), state_m.astype(F32), state_conv.astype(F32),
                                       b_s, t_s, t_s)
    x1, h2, pq = _outproj(xs, oa, hm, w_out_b, g1, row_vec(norm2_g), sc2, sh2, wq_b, tile_rows(b_s * t_s), False)
    y_s = peer_stage(pq, h2, x1, g2).reshape(b_s, t_s, dm)

    kv5 = lambda a, b, t: a.reshape((b, t) + kv_shape)
    kvw_p5, kvw_s5 = kv5(kvw_p, b_p, t_p), kv5(kvw_s, b_s, t_s)
    win_p = jnp.concatenate([jnp.zeros((b_p, n_win) + kv_shape, F32), kvw_p5], axis=1)[:, -n_win:]
    win_s = jnp.concatenate([state_win_kv.astype(F32), kvw_s5], axis=1)[:, -n_win:]
    return (y_p, y_s, kv5(kvc_p, b_p, t_p), kv5(kvc_s, b_s, t_s), kv5(kvs_p, b_p, t_p), kv5(kvs_s, b_s, t_s),
            win_p, win_s, c_p, c_s, n_p, n_s, m_p, m_s, conv_p, conv_s)
```
